```python
import jax, jax.numpy as jnp
from jax import lax
import numpy as np

D_MODEL = 1024
BATCH = 2
SEQ = 8192
DEPTH = 4

CHUNK = 64
Q_BLOCK = 128
EPS = 1e-6
N_MOD = 6

MLA_HEADS = 8
MLA_NOPE = 64
MLA_ROPE = 32
MLA_V = 64
Q_LORA = 256
KV_LORA = 128
ROPE_BASE = 10000.0

SB_HEADS = 8
SB_HEAD_DIM = 64

MLA_WIDTH = MLA_HEADS * MLA_V
SB_WIDTH = SB_HEADS * SB_HEAD_DIM
MIX_WIDTH = MLA_WIDTH + SB_WIDTH
IN_COLS = Q_LORA + KV_LORA + MLA_ROPE + 3 * SB_WIDTH
SPLIT_POINTS = (Q_LORA, Q_LORA + KV_LORA, Q_LORA + KV_LORA + MLA_ROPE,
                Q_LORA + KV_LORA + MLA_ROPE + SB_WIDTH,
                Q_LORA + KV_LORA + MLA_ROPE + 2 * SB_WIDTH)

D_FF = 2560
N_EXPERTS = 8
TOP_K = 2
D_FF_EXPERT = 3584
N_DENSE = (DEPTH + 1) // 2
N_MOE = DEPTH // 2

kernel_name = 'hymba_mla_stickbreaking_moe_adaln_chunk_causal'


def rms_norm(x, g):
    xf = x.astype(jnp.float32)
    y = xf * lax.rsqrt(jnp.mean(xf * xf, axis=-1, keepdims=True) + EPS)
    return (y * g.astype(jnp.float32)).astype(x.dtype)


def apply_rotary(x, cos, sin):
    half = x.shape[-1] // 2
    x1, x2 = x[..., :half], x[..., half:]
    cos = cos.astype(x.dtype)
    sin = sin.astype(x.dtype)
    return jnp.concatenate([x1 * cos - x2 * sin, x2 * cos + x1 * sin], axis=-1)


def swiglu(x, w_gate, w_up, w_down):
    return (jax.nn.silu(x @ w_gate) * (x @ w_up)) @ w_down


def sweep_query_blocks(q, block_fn):
    b, h, s, dq = q.shape
    nb = s // Q_BLOCK
    qb = jnp.moveaxis(q.reshape(b, h, nb, Q_BLOCK, dq), 2, 0)
    out = lax.map(lambda args: block_fn(args[0], args[1]), (qb, jnp.arange(nb)))
    return jnp.moveaxis(out, 0, 2).reshape(b, h, s, out.shape[-1])


def mla_attention(q, k, v):
    scale = (MLA_NOPE + MLA_ROPE) ** -0.5
    key_chunk = jnp.arange(k.shape[2]) // CHUNK

    def block(qblk, i):
        q_chunk = (i * Q_BLOCK + jnp.arange(Q_BLOCK)) // CHUNK
        allowed = key_chunk[None, :] <= q_chunk[:, None]
        logits = jnp.einsum('bhqd,bhkd->bhqk', qblk, k,
                            preferred_element_type=jnp.float32) * scale
        logits = jnp.where(allowed, logits, -jnp.inf)
        p = jax.nn.softmax(logits, axis=-1)
        return jnp.einsum('bhqk,bhkd->bhqd', p.astype(v.dtype), v)

    return sweep_query_blocks(q, block)


def stick_breaking_attention(q, k, v):
    scale = SB_HEAD_DIM ** -0.5
    key_idx = jnp.arange(k.shape[2])

    def block(qblk, i):
        q_idx = i * Q_BLOCK + jnp.arange(Q_BLOCK)
        strict = key_idx[None, :] < q_idx[:, None]
        z = jnp.einsum('bhqd,bhkd->bhqk', qblk, k,
                       preferred_element_type=jnp.float32) * scale
        log_keep = jnp.where(strict, jax.nn.log_sigmoid(-z), 0.0)
        key_axis = log_keep.ndim - 1
        log_stick = lax.cumsum(log_keep, axis=key_axis, reverse=True) - log_keep
        a = jnp.where(strict, jnp.exp(jax.nn.log_sigmoid(z) + log_stick), 0.0)
        return jnp.einsum('bhqk,bhkd->bhqd', a.astype(v.dtype), v)

    return sweep_query_blocks(q, block)


def token_mixer(h, cos, sin, w_in, g_q_lat, w_uq, g_kv_lat, w_ukv, g_grp_mla, g_grp_sb, w_out):
    b, s, _ = h.shape
    proj = h @ w_in
    c_q, c_kv, k_rope, q_sb, k_sb, v_sb = jnp.split(proj, SPLIT_POINTS, axis=-1)

    q = (rms_norm(c_q, g_q_lat) @ w_uq).reshape(b, s, MLA_HEADS, MLA_NOPE + MLA_ROPE)
    q_nope = q[..., :MLA_NOPE]
    q_rope = apply_rotary(q[..., MLA_NOPE:], cos[:, :, None, :], sin[:, :, None, :])
    kv = (rms_norm(c_kv, g_kv_lat) @ w_ukv).reshape(b, s, MLA_HEADS, MLA_NOPE + MLA_V)
    k_nope, v_mla = kv[..., :MLA_NOPE], kv[..., MLA_NOPE:]
    k_rope = apply_rotary(k_rope, cos, sin)
    q_mla = jnp.concatenate([q_nope, q_rope], axis=-1)
    k_mla = jnp.concatenate(
        [k_nope, jnp.broadcast_to(k_rope[:, :, None, :], (b, s, MLA_HEADS, MLA_ROPE))], axis=-1)
    o_mla = mla_attention(q_mla.transpose(0, 2, 1, 3), k_mla.transpose(0, 2, 1, 3),
                          v_mla.transpose(0, 2, 1, 3))
    o_mla = o_mla.transpose(0, 2, 1, 3).reshape(b, s, MLA_WIDTH)

    def heads(t):
        return t.reshape(b, s, SB_HEADS, SB_HEAD_DIM).transpose(0, 2, 1, 3)
    o_sb = stick_breaking_attention(heads(q_sb), heads(k_sb), heads(v_sb))
    o_sb = o_sb.transpose(0, 2, 1, 3).reshape(b, s, SB_WIDTH)

    merged = jnp.concatenate([rms_norm(o_mla, g_grp_mla), rms_norm(o_sb, g_grp_sb)], axis=-1)
    return merged @ w_out


def moe_swiglu(h, w_router, b_router, w_gate, w_up, w_down):
    b, s, d = h.shape
    t = h.reshape(b * s, d)
    logits = (t @ w_router + b_router).astype(jnp.float32)
    top_vals, top_idx = lax.top_k(logits, TOP_K)
    top_w = jax.nn.softmax(top_vals, axis=-1)
    combine = jnp.sum(jax.nn.one_hot(top_idx, N_EXPERTS, dtype=jnp.float32) * top_w[..., None],
                      axis=1).astype(t.dtype)
    out = jnp.zeros_like(t)
    for e in range(N_EXPERTS):
        out = out + combine[:, e:e + 1] * swiglu(t, w_gate[e], w_up[e], w_down[e])
    return out.reshape(b, s, d)


def setup_inputs(seed: int = 0) -> dict:
    key = jax.random.key(seed)
    ks = jax.random.split(key, 24)
    f32 = jnp.float32

    def nrm(k, shape, scale):
        return jax.random.normal(k, shape, f32) * scale

    def gain(k, shape):
        return 1.0 + 0.1 * jax.random.normal(k, shape, f32)

    offset = jax.random.randint(ks[2], (BATCH, 1), 0, 64) * CHUNK
    positions = (offset + jnp.arange(SEQ, dtype=jnp.int32)[None, :]).astype(jnp.int32)
    return {
        'x': nrm(ks[0], (BATCH, SEQ, D_MODEL), 1.0),
        'c': nrm(ks[1], (BATCH, D_MODEL), 1.0),
        'positions': positions,
        'w_ada': nrm(ks[3], (DEPTH, D_MODEL, N_MOD * D_MODEL), 0.5 * D_MODEL ** -0.5),
        'b_ada': nrm(ks[4], (DEPTH, N_MOD * D_MODEL), 0.02),
        'g_mix': gain(ks[5], (DEPTH, D_MODEL)),
        'g_ffn': gain(ks[6], (DEPTH, D_MODEL)),
        'w_in': nrm(ks[7], (DEPTH, D_MODEL, IN_COLS), D_MODEL ** -0.5),
        'g_q_lat': gain(ks[8], (DEPTH, Q_LORA)),
        'w_uq': nrm(ks[9], (DEPTH, Q_LORA, MLA_HEADS * (MLA_NOPE + MLA_ROPE)), Q_LORA ** -0.5),
        'g_kv_lat': gain(ks[10], (DEPTH, KV_LORA)),
        'w_ukv': nrm(ks[11], (DEPTH, KV_LORA, MLA_HEADS * (MLA_NOPE + MLA_V)), KV_LORA ** -0.5),
        'g_grp_mla': gain(ks[12], (DEPTH, MLA_WIDTH)),
        'g_grp_sb': gain(ks[13], (DEPTH, SB_WIDTH)),
        'w_out': nrm(ks[14], (DEPTH, MIX_WIDTH, D_MODEL), MIX_WIDTH ** -0.5),
        'w_ff_gate': nrm(ks[15], (N_DENSE, D_MODEL, D_FF), D_MODEL ** -0.5),
        'w_ff_up': nrm(ks[16], (N_DENSE, D_MODEL, D_FF), D_MODEL ** -0.5),
        'w_ff_down': nrm(ks[17], (N_DENSE, D_FF, D_MODEL), D_FF ** -0.5),
        'w_router': nrm(ks[18], (N_MOE, D_MODEL, N_EXPERTS), D_MODEL ** -0.5),
        'b_router': nrm(ks[19], (N_MOE, N_EXPERTS), 0.01),
        'w_ex_gate': nrm(ks[20], (N_MOE, N_EXPERTS, D_MODEL, D_FF_EXPERT), D_MODEL ** -0.5),
        'w_ex_up': nrm(ks[21], (N_MOE, N_EXPERTS, D_MODEL, D_FF_EXPERT), D_MODEL ** -0.5),
        'w_ex_down': nrm(ks[22], (N_MOE, N_EXPERTS, D_FF_EXPERT, D_MODEL), D_FF_EXPERT ** -0.5),
        'g_final': gain(ks[23], (D_MODEL,)),
    }


def reference(x, c, positions, w_ada, b_ada, g_mix, g_ffn, w_in, g_q_lat, w_uq, g_kv_lat,
              w_ukv, g_grp_mla, g_grp_sb, w_out, w_ff_gate, w_ff_up, w_ff_down, w_router,
              b_router, w_ex_gate, w_ex_up, w_ex_down, g_final):
    half = MLA_ROPE // 2
    inv_freq = ROPE_BASE ** (-jnp.arange(half, dtype=jnp.float32) / half)
    ang = positions.astype(jnp.float32)[..., None] * inv_freq
    cos, sin = jnp.cos(ang), jnp.sin(ang)
    c_act = jax.nn.silu(c)

    for layer in range(DEPTH):
        mod = c_act @ w_ada[layer] + b_ada[layer]
        shift_m, scale_m, gate_m, shift_f, scale_f, gate_f = [
            m[:, None, :] for m in jnp.split(mod, N_MOD, axis=-1)]

        h = rms_norm(x, g_mix[layer]) * (1.0 + scale_m) + shift_m
        mixed = token_mixer(h, cos, sin, w_in[layer], g_q_lat[layer], w_uq[layer],
                            g_kv_lat[layer], w_ukv[layer], g_grp_mla[layer], g_grp_sb[layer],
                            w_out[layer])
        x = x + gate_m * mixed

        h = rms_norm(x, g_ffn[layer]) * (1.0 + scale_f) + shift_f
        if layer % 2 == 0:
            i = layer // 2
            y = swiglu(h, w_ff_gate[i], w_ff_up[i], w_ff_down[i])
        else:
            i = layer // 2
            y = moe_swiglu(h, w_router[i], b_router[i], w_ex_gate[i], w_ex_up[i], w_ex_down[i])
        x = x + gate_f * y

    return rms_norm(x, g_final)
```

```python
import functools
import math

import jax
import jax.numpy as jnp
from jax import lax
from jax.experimental import pallas as pl
from jax.experimental.pallas import tpu as pltpu

F32 = jnp.float32
BF16 = jnp.bfloat16

EPS = 1e-6
N_MOD = 6
CHUNK = 64
MLA_HEADS = 8
MLA_NOPE = 64
MLA_ROPE = 32
MLA_V = 64
Q_LORA = 256
KV_LORA = 128
ROPE_BASE = 10000.0
SB_HEADS = 8
SB_HEAD_DIM = 64
N_EXPERTS = 8
TOP_K = 2

LANES = 128
HALF_ROPE = MLA_ROPE // 2
MLA_QK_DIM = MLA_NOPE + MLA_ROPE
LOG2E = 1.4426950408889634
NEG_BIG = -1e30

ROW_TILE = 512
ATT_TILE = 256
EXPERT_ROW_TILE = 1024
EXPERT_FF_TILE = 512
GATHER_TILE = 1024
FFN_CHUNK = 512


def _split_bf16(v):
    hi = v.astype(BF16)
    lo = (v - hi.astype(F32)).astype(BF16)
    return hi, lo


def _dot(a, b):
    return jnp.dot(a, b, preferred_element_type=F32)


def _rms_rows(v):
    return v * lax.rsqrt(jnp.mean(v * v, axis=-1, keepdims=True) + EPS)


def _modulated_norm(x, gain, scale, shift):
    return _rms_rows(x) * (gain * (1.0 + scale)) + shift


def _ada_kernel(c_ref, w_ref, b_ref, o_ref):
    c = c_ref[...]
    act = c * (1.0 / (1.0 + jnp.exp(-c)))
    a_hi, a_lo = _split_bf16(act)
    w_hi, w_lo = _split_bf16(w_ref[0])
    o_ref[0] = _dot(a_hi, w_hi) + _dot(a_lo, w_hi) + _dot(a_hi, w_lo) + b_ref[0]


def _ada_call(c_pad, w_ada, b_ada):
    depth, d, n = w_ada.shape
    tn = n // 4
    return pl.pallas_call(
        _ada_kernel,
        grid=(depth, n // tn),
        in_specs=[
            pl.BlockSpec((8, d), lambda l, j: (0, 0)),
            pl.BlockSpec((1, d, tn), lambda l, j: (l, 0, j)),
            pl.BlockSpec((1, 1, tn), lambda l, j: (l, 0, j)),
        ],
        out_specs=pl.BlockSpec((1, 8, tn), lambda l, j: (l, 0, j)),
        out_shape=jax.ShapeDtypeStruct((depth, 8, n), F32),
        name="ada_mod",
    )(c_pad, w_ada, b_ada.reshape(depth, 1, n))


def _rope_table_kernel(pos_ref, freq_ref, cos_ref, sin_ref):
    ang = pos_ref[0] * freq_ref[...]
    lane = lax.broadcasted_iota(jnp.int32, ang.shape, 1)
    in_rope = (lane >= MLA_NOPE) & (lane < MLA_QK_DIM)
    first_half = lane < MLA_NOPE + HALF_ROPE
    cos_ref[0] = jnp.where(in_rope, jnp.cos(ang), jnp.where(lane < MLA_NOPE, 1.0, 0.0))
    s = jnp.sin(ang)
    sin_ref[0] = jnp.where(in_rope, jnp.where(first_half, -s, s), 0.0)


def _rope_tables(pos_lanes, freq_lanes):
    b, s, _ = pos_lanes.shape
    tm = ROW_TILE
    spec = pl.BlockSpec((1, tm, LANES), lambda bi, i: (bi, i, 0))
    return pl.pallas_call(
        _rope_table_kernel,
        grid=(b, s // tm),
        in_specs=[spec, pl.BlockSpec((1, LANES), lambda bi, i: (0, 0))],
        out_specs=[spec, spec],
        out_shape=[jax.ShapeDtypeStruct((b, s, LANES), F32)] * 2,
        name="rope_tables",
    )(pos_lanes, freq_lanes)


IN_OFF_CQ = 0
IN_OFF_CKV = Q_LORA
IN_OFF_KR = IN_OFF_CKV + KV_LORA
IN_OFF_KRS = IN_OFF_KR + LANES
IN_OFF_QSB = IN_OFF_KRS + LANES
SB_WIDTH = SB_HEADS * SB_HEAD_DIM
MLA_WIDTH = MLA_HEADS * MLA_V
IN_OFF_KSB = IN_OFF_QSB + SB_WIDTH
IN_OFF_VSB = IN_OFF_KSB + SB_WIDTH
IN_COLS_PADDED = IN_OFF_VSB + SB_WIDTH
MLA_PAD_WIDTH = MLA_HEADS * LANES


def _proj_kernel(x_ref, mod_ref, gmix_ref, cos_ref, sin_ref, win_ref, gq_ref, wuq_ref, gkv_ref, wukv_ref,
                 qm_ref, km_ref, vtm_ref, qs_ref, ks_ref, vts_ref):
    x = x_ref[0]
    mod = mod_ref[0]
    h = _modulated_norm(x, gmix_ref[...], mod[1:2], mod[0:1])
    proj = _dot(h.astype(BF16), win_ref[...])
    cos = cos_ref[0]
    sin = sin_ref[0]
    n_kv = vtm_ref.shape[1]
    tk = vtm_ref.shape[3]

    cqn = _rms_rows(proj[:, IN_OFF_CQ:IN_OFF_CQ + Q_LORA]) * gq_ref[...]
    q2 = _dot(cqn.astype(BF16), wuq_ref[...])
    q_scale = MLA_QK_DIM ** -0.5 * LOG2E
    for hd in range(MLA_HEADS):
        lo = hd * LANES
        q = q2[:, lo:lo + LANES] * cos + q2[:, MLA_PAD_WIDTH + lo:MLA_PAD_WIDTH + lo + LANES] * sin
        qm_ref[0, :, lo:lo + LANES] = (q * q_scale).astype(qm_ref.dtype)

    ckvn = _rms_rows(proj[:, IN_OFF_CKV:IN_OFF_CKV + KV_LORA]) * gkv_ref[...]
    kv2 = _dot(ckvn.astype(BF16), wukv_ref[...])
    k_rope = proj[:, IN_OFF_KR:IN_OFF_KR + LANES] * cos + proj[:, IN_OFF_KRS:IN_OFF_KRS + LANES] * sin
    for hd in range(MLA_HEADS):
        lo = hd * LANES
        km_ref[0, :, lo:lo + LANES] = (kv2[:, lo:lo + LANES] + k_rope).astype(km_ref.dtype)
    v_t = kv2[:, MLA_PAD_WIDTH:].T
    for c in range(n_kv):
        vtm_ref[0, c] = v_t[:, c * tk:(c + 1) * tk].astype(vtm_ref.dtype)

    qs_ref[0] = proj[:, IN_OFF_QSB:IN_OFF_QSB + SB_WIDTH].astype(qs_ref.dtype)
    ks_ref[0] = proj[:, IN_OFF_KSB:IN_OFF_KSB + SB_WIDTH].astype(ks_ref.dtype)
    vs_t = proj[:, IN_OFF_VSB:IN_OFF_VSB + SB_WIDTH].T
    for c in range(n_kv):
        vts_ref[0, c] = vs_t[:, c * tk:(c + 1) * tk].astype(vts_ref.dtype)


def _proj_call(x, mod, gmix, cos_t, sin_t, win, gq, wuq, gkv, wukv):
    b, s, d = x.shape
    tm = ROW_TILE
    tk = ATT_TILE
    n_kv = tm // tk
    row = lambda w: pl.BlockSpec((1, tm, w), lambda bi, i: (bi, i, 0))
    full = lambda a: pl.BlockSpec(a.shape, lambda bi, i: (0,) * a.ndim)
    vt_spec = lambda w: pl.BlockSpec((1, n_kv, w, tk), lambda bi, i: (bi, i, 0, 0))
    return pl.pallas_call(
        _proj_kernel,
        grid=(b, s // tm),
        in_specs=[row(d), pl.BlockSpec((1, N_MOD, d), lambda bi, i: (bi, 0, 0)), full(gmix), row(LANES), row(LANES),
                  full(win), full(gq), full(wuq), full(gkv), full(wukv)],
        out_specs=[row(MLA_PAD_WIDTH), row(MLA_PAD_WIDTH), vt_spec(MLA_WIDTH),
                   row(SB_WIDTH), row(SB_WIDTH), vt_spec(SB_WIDTH)],
        out_shape=[
            jax.ShapeDtypeStruct((b, s, MLA_PAD_WIDTH), BF16),
            jax.ShapeDtypeStruct((b, s, MLA_PAD_WIDTH), BF16),
            jax.ShapeDtypeStruct((b, s // tk, MLA_WIDTH, tk), BF16),
            jax.ShapeDtypeStruct((b, s, SB_WIDTH), BF16),
            jax.ShapeDtypeStruct((b, s, SB_WIDTH), BF16),
            jax.ShapeDtypeStruct((b, s // tk, SB_WIDTH, tk), BF16),
        ],
        name="in_proj",
    )(x, mod, gmix, cos_t, sin_t, win, gq, wuq, gkv, wukv)


def _mla_kernel(q_ref, k_ref, vt_ref, o_ref):
    tq = q_ref.shape[1]
    tk = vt_ref.shape[3]
    i = pl.program_id(2)
    q_t = q_ref[0].astype(F32).T.astype(BF16)

    def step(j, carry, masked):
        m, l, acc = carry
        start = pl.multiple_of(j * tk, tk)
        s = _dot(k_ref[0, pl.ds(start, tk), :], q_t)
        if masked:
            key_chunk = lax.broadcasted_iota(jnp.int32, s.shape, 0) // CHUNK
            qry_chunk = lax.broadcasted_iota(jnp.int32, s.shape, 1) // CHUNK
            s = jnp.where(key_chunk <= qry_chunk, s, NEG_BIG)
        m_new = jnp.maximum(m, jnp.max(s, axis=0, keepdims=True))
        alpha = jnp.exp2(m - m_new)
        p = jnp.exp2(s - m_new)
        l = alpha * l + jnp.sum(p, axis=0, keepdims=True)
        acc = alpha * acc + _dot(vt_ref[0, j], p.astype(BF16))
        return m_new, l, acc

    init = (jnp.full((1, tq), NEG_BIG, F32), jnp.zeros((1, tq), F32), jnp.zeros((o_ref.shape[1], tq), F32))
    carry = lax.fori_loop(0, i, functools.partial(step, masked=False), init)
    _, l, acc = step(i, carry, masked=True)
    o_ref[0] = acc * (1.0 / l)


def _mla_call(q, k, vt):
    b, s, _ = q.shape
    n_kv, tk = vt.shape[1], vt.shape[3]
    tq = tk
    return pl.pallas_call(
        _mla_kernel,
        grid=(b, MLA_HEADS, s // tq),
        in_specs=[
            pl.BlockSpec((1, tq, LANES), lambda bi, h, i: (bi, i, h)),
            pl.BlockSpec((1, s, LANES), lambda bi, h, i: (bi, 0, h)),
            pl.BlockSpec((1, n_kv, MLA_V, tk), lambda bi, h, i: (bi, 0, h, 0)),
        ],
        out_specs=pl.BlockSpec((1, MLA_V, tq), lambda bi, h, i: (bi, h, i)),
        out_shape=jax.ShapeDtypeStruct((b, MLA_WIDTH, s), F32),
        name="mla_attention",
    )(q, k, vt)


def _sb_kernel(q_ref, k_ref, vt_ref, o_ref):
    tq = q_ref.shape[1]
    tk = vt_ref.shape[3]
    i = pl.program_id(2)
    q_t = q_ref[0].astype(F32).T
    head_row = lax.broadcasted_iota(jnp.int32, q_t.shape, 0) // SB_HEAD_DIM
    r = lax.broadcasted_iota(jnp.int32, (tk, tk), 0)
    c = lax.broadcasted_iota(jnp.int32, (tk, tk), 1)
    later_keys = jnp.where(c > r, 1.0, 0.0).astype(BF16)

    for hd in range(2):
        q_h = jnp.where(head_row == hd, q_t, 0.0).astype(BF16)
        v_lo = hd * SB_HEAD_DIM

        def step(j, carry, masked):
            run, acc = carry
            start = pl.multiple_of(j * tk, tk)
            z = _dot(k_ref[0, pl.ds(start, tk), :], q_h)
            soft = jnp.log(1.0 + jnp.exp(-jnp.abs(z)))
            log_keep = -jnp.maximum(z, 0.0) - soft
            log_beta = jnp.minimum(z, 0.0) - soft
            if masked:
                strict = (lax.broadcasted_iota(jnp.int32, z.shape, 0) < lax.broadcasted_iota(jnp.int32, z.shape, 1))
                log_keep = jnp.where(strict, log_keep, 0.0)
            keep_hi, keep_lo = _split_bf16(log_keep)
            later = _dot(later_keys, keep_hi) + _dot(later_keys, keep_lo)
            a = jnp.exp(log_beta + later + run)
            if masked:
                a = jnp.where(strict, a, 0.0)
            acc = acc + _dot(vt_ref[0, j, v_lo:v_lo + SB_HEAD_DIM, :], a.astype(BF16))
            run = run + jnp.sum(log_keep, axis=0, keepdims=True)
            return run, acc

        init = (jnp.zeros((1, tq), F32), jnp.zeros((SB_HEAD_DIM, tq), F32))
        carry = step(i, init, masked=True)
        _, acc = lax.fori_loop(0, i, lambda t, cr: step(i - 1 - t, cr, masked=False), carry)
        o_ref[0, v_lo:v_lo + SB_HEAD_DIM, :] = acc


def _sb_call(q, k, vt):
    b, s, _ = q.shape
    n_kv, tk = vt.shape[1], vt.shape[3]
    tq = tk
    return pl.pallas_call(
        _sb_kernel,
        grid=(b, SB_HEADS // 2, s // tq),
        in_specs=[
            pl.BlockSpec((1, tq, LANES), lambda bi, p, i: (bi, i, p)),
            pl.BlockSpec((1, s, LANES), lambda bi, p, i: (bi, 0, p)),
            pl.BlockSpec((1, n_kv, LANES, tk), lambda bi, p, i: (bi, 0, p, 0)),
        ],
        out_specs=pl.BlockSpec((1, LANES, tq), lambda bi, p, i: (bi, p, i)),
        out_shape=jax.ShapeDtypeStruct((b, SB_WIDTH, s), F32),
        name="sb_attention",
    )(q, k, vt)


def _merge_kernel(x_ref, om_ref, os_ref, mod_ref, gm_ref, gs_ref, wout_ref, o_ref):
    def group(o_t, gain):
        inv = lax.rsqrt(jnp.mean(o_t * o_t, axis=0, keepdims=True) + EPS)
        return ((o_t * inv).T * gain).astype(BF16)

    a = group(om_ref[0], gm_ref[...])
    bb = group(os_ref[0], gs_ref[...])
    mixed = _dot(a, wout_ref[:MLA_WIDTH, :]) + _dot(bb, wout_ref[MLA_WIDTH:, :])
    o_ref[0] = x_ref[0] + mod_ref[0][2:3] * mixed


def _merge_call(x, o_mla, o_sb, mod, g_mla, g_sb, wout):
    b, s, d = x.shape
    tm = ROW_TILE
    row = pl.BlockSpec((1, tm, d), lambda bi, i: (bi, i, 0))
    col = lambda w: pl.BlockSpec((1, w, tm), lambda bi, i: (bi, 0, i))
    full = lambda a: pl.BlockSpec(a.shape, lambda bi, i: (0,) * a.ndim)
    return pl.pallas_call(
        _merge_kernel,
        grid=(b, s // tm),
        in_specs=[row, col(MLA_WIDTH), col(SB_WIDTH), pl.BlockSpec((1, N_MOD, d), lambda bi, i: (bi, 0, 0)),
                  full(g_mla), full(g_sb), full(wout)],
        out_specs=row,
        out_shape=jax.ShapeDtypeStruct((b, s, d), F32),
        name="merge_out_proj",
    )(x, o_mla, o_sb, mod, g_mla, g_sb, wout)


def _silu(v):
    return v * (1.0 / (1.0 + jnp.exp(-v)))


def _ffn_kernel(x_ref, mod_ref, g_ref, wg_ref, wu_ref, wd_ref, o_ref):
    x = x_ref[0]
    mod = mod_ref[0]
    hb = _modulated_norm(x, g_ref[...], mod[4:5], mod[3:4]).astype(BF16)
    acc = jnp.zeros(x.shape, F32)
    for c in range(wg_ref.shape[1] // FFN_CHUNK):
        lo = c * FFN_CHUNK
        gate = _dot(hb, wg_ref[:, lo:lo + FFN_CHUNK])
        up = _dot(hb, wu_ref[:, lo:lo + FFN_CHUNK])
        acc = acc + _dot((_silu(gate) * up).astype(BF16), wd_ref[lo:lo + FFN_CHUNK, :])
    o_ref[0] = x + mod[5:6] * acc


def _ffn_call(x, mod, g, wg, wu, wd):
    b, s, d = x.shape
    tm = ROW_TILE
    row = pl.BlockSpec((1, tm, d), lambda bi, i: (bi, i, 0))
    full = lambda a: pl.BlockSpec(a.shape, lambda bi, i: (0,) * a.ndim)
    return pl.pallas_call(
        _ffn_kernel,
        grid=(b, s // tm),
        in_specs=[row, pl.BlockSpec((1, N_MOD, d), lambda bi, i: (bi, 0, 0)), full(g), full(wg), full(wu), full(wd)],
        out_specs=row,
        out_shape=jax.ShapeDtypeStruct((b, s, d), F32),
        name="dense_ffn",
    )(x, mod, g, wg, wu, wd)


def _router_kernel(x_ref, mod_ref, g_ref, wr_hi_ref, wr_lo_ref, br_ref, h_ref, info_ref):
    x = x_ref[0]
    mod = mod_ref[0]
    h = _modulated_norm(x, g_ref[...], mod[4:5], mod[3:4])
    h_ref[0] = h
    h_hi, h_lo = _split_bf16(h)
    w_hi = wr_hi_ref[...]
    logits = _dot(h_hi, w_hi) + _dot(h_lo, w_hi) + _dot(h_hi, wr_lo_ref[...]) + br_ref[...]
    lane = lax.broadcasted_iota(jnp.int32, logits.shape, 1)
    logits = jnp.where(lane < N_EXPERTS, logits, NEG_BIG)
    m1 = jnp.max(logits, axis=-1, keepdims=True)
    i1 = jnp.min(jnp.where(logits == m1, lane, LANES), axis=-1, keepdims=True)
    rest = jnp.where(lane == i1, NEG_BIG, logits)
    m2 = jnp.max(rest, axis=-1, keepdims=True)
    i2 = jnp.min(jnp.where(rest == m2, lane, LANES), axis=-1, keepdims=True)
    e = jnp.exp(m2 - m1)
    w1 = 1.0 / (1.0 + e)
    w2 = e * w1
    info = jnp.where(lane == 0, i1.astype(F32),
                     jnp.where(lane == 1, i2.astype(F32), jnp.where(lane == 2, w1, jnp.where(lane == 3, w2, 0.0))))
    info_ref[0] = info


def _router_call(x, mod, g, wr_hi, wr_lo, br):
    b, s, d = x.shape
    tm = ROW_TILE
    row = pl.BlockSpec((1, tm, d), lambda bi, i: (bi, i, 0))
    full = lambda a: pl.BlockSpec(a.shape, lambda bi, i: (0,) * a.ndim)
    return pl.pallas_call(
        _router_kernel,
        grid=(b, s // tm),
        in_specs=[row, pl.BlockSpec((1, N_MOD, d), lambda bi, i: (bi, 0, 0)), full(g), full(wr_hi), full(wr_lo), full(br)],
        out_specs=[row, pl.BlockSpec((1, tm, LANES), lambda bi, i: (bi, i, 0))],
        out_shape=[jax.ShapeDtypeStruct((b, s, d), F32), jax.ShapeDtypeStruct((b, s, LANES), F32)],
        name="moe_router",
    )(x, mod, g, wr_hi, wr_lo, br)


def _row_copy(src_hbm, dst_vmem, sem, src_row, dst_row):
    return pltpu.make_async_copy(src_hbm.at[pl.ds(src_row, 1), :], dst_vmem.at[pl.ds(dst_row, 1), :], sem)


def _gather_rows(idx_ref, src_hbm, dst_vmem, sem, n_rows):
    def start(r, _):
        _row_copy(src_hbm, dst_vmem, sem, idx_ref[r], r).start()
        return 0

    def wait(r, _):
        _row_copy(src_hbm, dst_vmem, sem, 0, r).wait()
        return 0

    lax.fori_loop(0, n_rows, start, 0)
    lax.fori_loop(0, n_rows, wait, 0)


def _gather_kernel(idx_ref, src_ref, o_ref, sem):
    _gather_rows(idx_ref, src_ref, o_ref, sem, o_ref.shape[0])


def _gather_call(row_token, h_flat):
    n_rows = row_token.shape[0]
    d = h_flat.shape[1]
    tg = GATHER_TILE
    return pl.pallas_call(
        _gather_kernel,
        grid=(n_rows // tg,),
        in_specs=[pl.BlockSpec((tg,), lambda i: (i,), memory_space=pltpu.SMEM), pl.BlockSpec(memory_space=pl.ANY)],
        out_specs=pl.BlockSpec((tg, d), lambda i: (i, 0)),
        out_shape=jax.ShapeDtypeStruct((n_rows, d), h_flat.dtype),
        scratch_shapes=[pltpu.SemaphoreType.DMA(())],
        name="moe_gather",
    )(row_token, h_flat)


def _expert_kernel(tile_expert_ref, n_valid_ref, xs_ref, wg_ref, wu_ref, wd_ref, o_ref):
    i = pl.program_id(0)
    j = pl.program_id(1)

    @pl.when(i < n_valid_ref[0])
    def _():
        xb = xs_ref[...].astype(BF16)
        gate = _dot(xb, wg_ref[0, 0].astype(BF16))
        up = _dot(xb, wu_ref[0, 0].astype(BF16))
        part = _dot((_silu(gate) * up).astype(BF16), wd_ref[0, 0].astype(BF16))

        @pl.when(j == 0)
        def _():
            o_ref[...] = part

        @pl.when(j > 0)
        def _():
            o_ref[...] += part

    @pl.when((i >= n_valid_ref[0]) & (j == 0))
    def _():
        o_ref[...] = jnp.zeros(o_ref.shape, o_ref.dtype)


def _expert_call(tile_expert, n_valid, xs, wg, wu, wd, layer_idx):
    n_rows, d = xs.shape
    tm = EXPERT_ROW_TILE
    tf = EXPERT_FF_TILE
    n_ff = wg.shape[-1]

    def tile(i, te, nv):
        return jnp.minimum(i, nv[0] - 1)

    grid_spec = pltpu.PrefetchScalarGridSpec(
        num_scalar_prefetch=2,
        grid=(n_rows // tm, n_ff // tf),
        in_specs=[
            pl.BlockSpec((tm, d), lambda i, j, te, nv: (tile(i, te, nv), 0)),
            pl.BlockSpec((1, 1, d, tf), lambda i, j, te, nv: (layer_idx, te[tile(i, te, nv)], 0, jnp.where(i < nv[0], j, n_ff // tf - 1))),
            pl.BlockSpec((1, 1, d, tf), lambda i, j, te, nv: (layer_idx, te[tile(i, te, nv)], 0, jnp.where(i < nv[0], j, n_ff // tf - 1))),
            pl.BlockSpec((1, 1, tf, d), lambda i, j, te, nv: (layer_idx, te[tile(i, te, nv)], jnp.where(i < nv[0], j, n_ff // tf - 1), 0)),
        ],
        out_specs=pl.BlockSpec((tm, d), lambda i, j, te, nv: (i, 0)),
    )
    return pl.pallas_call(
        _expert_kernel,
        grid_spec=grid_spec,
        out_shape=jax.ShapeDtypeStruct((n_rows, d), F32),
        name="moe_experts",
    )(tile_expert, n_valid, xs, wg, wu, wd)


def _combine_kernel(p1_ref, p2_ref, x_ref, info_ref, gate_ref, gfin_ref, y_ref, o_ref, buf1, buf2, sem1, sem2, *, final_norm):
    n = x_ref.shape[0]

    def start(r, _):
        _row_copy(y_ref, buf1, sem1, p1_ref[r], r).start()
        _row_copy(y_ref, buf2, sem2, p2_ref[r], r).start()
        return 0

    def wait(r, _):
        _row_copy(y_ref, buf1, sem1, 0, r).wait()
        _row_copy(y_ref, buf2, sem2, 0, r).wait()
        return 0

    lax.fori_loop(0, n, start, 0)
    lax.fori_loop(0, n, wait, 0)
    info = info_ref[...]
    y = info[:, 2:3] * buf1[...] + info[:, 3:4] * buf2[...]
    out = x_ref[...] + gate_ref[0] * y
    if final_norm:
        out = _rms_rows(out) * gfin_ref[...]
    o_ref[...] = out


def _combine_call(p1, p2, x_flat, info_flat, gate_rows, g_final, y_sorted, tokens_per_batch, final_norm):
    t, d = x_flat.shape
    tc = GATHER_TILE
    per_b = tokens_per_batch // tc
    idx = pl.BlockSpec((tc,), lambda i: (i,), memory_space=pltpu.SMEM)
    return pl.pallas_call(
        functools.partial(_combine_kernel, final_norm=final_norm),
        grid=(t // tc,),
        in_specs=[idx, idx, pl.BlockSpec((tc, d), lambda i: (i, 0)), pl.BlockSpec((tc, LANES), lambda i: (i, 0)),
                  pl.BlockSpec((1, 1, d), lambda i: (i // per_b, 0, 0)), pl.BlockSpec((1, d), lambda i: (0, 0)),
                  pl.BlockSpec(memory_space=pl.ANY)],
        out_specs=pl.BlockSpec((tc, d), lambda i: (i, 0)),
        out_shape=jax.ShapeDtypeStruct((t, d), F32),
        scratch_shapes=[pltpu.VMEM((tc, d), F32), pltpu.VMEM((tc, d), F32),
                        pltpu.SemaphoreType.DMA(()), pltpu.SemaphoreType.DMA(())],
        name="moe_combine",
    )(p1, p2, x_flat, info_flat, gate_rows, g_final, y_sorted)


def _routing_tables(expert_ids, n_tokens):
    tm = EXPERT_ROW_TILE
    n_assign = n_tokens * TOP_K
    n_rows = n_assign + N_EXPERTS * tm
    e_flat = expert_ids.reshape(n_assign)
    onehot = (e_flat[:, None] == jnp.arange(N_EXPERTS, dtype=jnp.int32)[None, :]).astype(jnp.int32)
    rank = jnp.sum((jnp.cumsum(onehot, axis=0) - onehot) * onehot, axis=1)
    counts = jnp.sum(onehot, axis=0)
    padded = ((counts + tm - 1) // tm) * tm
    ends = jnp.cumsum(padded)
    starts = ends - padded
    pos = (starts[e_flat] + rank).astype(jnp.int32)
    row_token = jnp.zeros((n_rows,), jnp.int32).at[pos].set(jnp.arange(n_assign, dtype=jnp.int32) // TOP_K)
    n_valid = (ends[-1] // tm).astype(jnp.int32).reshape(1)
    tile_start = jnp.arange(n_rows // tm, dtype=jnp.int32) * tm
    tile_expert = jnp.minimum(jnp.searchsorted(ends, tile_start, side="right"), N_EXPERTS - 1).astype(jnp.int32)
    pos2 = pos.reshape(n_tokens, TOP_K)
    return row_token, tile_expert, n_valid, pos2[:, 0], pos2[:, 1]


def _prep_w_in(w_in):
    d = w_in.shape[0]
    cq, ckv, kr, qsb, ksb, vsb = jnp.split(
        w_in, (Q_LORA, Q_LORA + KV_LORA, Q_LORA + KV_LORA + MLA_ROPE,
               Q_LORA + KV_LORA + MLA_ROPE + SB_WIDTH, Q_LORA + KV_LORA + MLA_ROPE + 2 * SB_WIDTH), axis=1)
    z = lambda n: jnp.zeros((d, n), w_in.dtype)
    kr_blk = jnp.concatenate([z(MLA_NOPE), kr, z(LANES - MLA_QK_DIM)], axis=1)
    kr_swap = jnp.concatenate([z(MLA_NOPE), kr[:, HALF_ROPE:], kr[:, :HALF_ROPE], z(LANES - MLA_QK_DIM)], axis=1)
    return jnp.concatenate([cq, ckv, kr_blk, kr_swap, qsb * SB_HEAD_DIM ** -0.5, ksb, vsb], axis=1).astype(BF16)


def _prep_w_uq(w_uq):
    r = w_uq.shape[0]
    w = w_uq.reshape(r, MLA_HEADS, MLA_QK_DIM)
    nope, x1, x2 = w[..., :MLA_NOPE], w[..., MLA_NOPE:MLA_NOPE + HALF_ROPE], w[..., MLA_NOPE + HALF_ROPE:]
    pad = jnp.zeros((r, MLA_HEADS, LANES - MLA_QK_DIM), w.dtype)
    plain = jnp.concatenate([nope, x1, x2, pad], axis=-1).reshape(r, MLA_PAD_WIDTH)
    swapped = jnp.concatenate([jnp.zeros_like(nope), x2, x1, pad], axis=-1).reshape(r, MLA_PAD_WIDTH)
    return jnp.concatenate([plain, swapped], axis=1).astype(BF16)


def _prep_w_ukv(w_ukv):
    r = w_ukv.shape[0]
    w = w_ukv.reshape(r, MLA_HEADS, MLA_NOPE + MLA_V)
    k_nope, v = w[..., :MLA_NOPE], w[..., MLA_NOPE:]
    k_pad = jnp.concatenate([k_nope, jnp.zeros((r, MLA_HEADS, LANES - MLA_NOPE), w.dtype)], axis=-1)
    return jnp.concatenate([k_pad.reshape(r, MLA_PAD_WIDTH), v.reshape(r, MLA_WIDTH)], axis=1).astype(BF16)


def kernel(x, c, positions, w_ada, b_ada, g_mix, g_ffn, w_in, g_q_lat, w_uq, g_kv_lat, w_ukv, g_grp_mla, g_grp_sb, w_out, w_ff_gate, w_ff_up, w_ff_down, w_router, b_router, w_ex_gate, w_ex_up, w_ex_down, g_final):
    b, s, d = x.shape
    depth = w_ada.shape[0]
    n_tokens = b * s
    assert depth % 2 == 0, "the final norm is fused into the routed mixer of the last layer"

    c_pad = jnp.zeros((8, d), F32).at[:b].set(c)
    mod_all = _ada_call(c_pad, w_ada, b_ada)[:, :b].reshape(depth, b, N_MOD, d)

    lane = jnp.arange(LANES)
    freq = ROPE_BASE ** (-((lane - MLA_NOPE) % HALF_ROPE).astype(F32) / HALF_ROPE)
    pos_lanes = jnp.broadcast_to(positions.astype(F32)[..., None], (b, s, LANES))
    cos_t, sin_t = _rope_tables(pos_lanes, freq.reshape(1, LANES))

    row2 = lambda v: v.reshape(1, -1)
    for layer in range(depth):
        mod = mod_all[layer]
        qm, km, vtm, qs, ks, vts = _proj_call(
            x, mod, row2(g_mix[layer]), cos_t, sin_t, _prep_w_in(w_in[layer]), row2(g_q_lat[layer]),
            _prep_w_uq(w_uq[layer]), row2(g_kv_lat[layer]), _prep_w_ukv(w_ukv[layer]))
        o_mla = _mla_call(qm, km, vtm)
        o_sb = _sb_call(qs, ks, vts)
        x = _merge_call(x, o_mla, o_sb, mod, row2(g_grp_mla[layer]), row2(g_grp_sb[layer]), w_out[layer].astype(BF16))

        i = layer // 2
        if layer % 2 == 0:
            x = _ffn_call(x, mod, row2(g_ffn[layer]), w_ff_gate[i].astype(BF16), w_ff_up[i].astype(BF16),
                          w_ff_down[i].astype(BF16))
        else:
            wr = jnp.zeros((d, LANES), F32).at[:, :N_EXPERTS].set(w_router[i])
            wr_hi = wr.astype(BF16)
            wr_lo = (wr - wr_hi.astype(F32)).astype(BF16)
            br = jnp.zeros((1, LANES), F32).at[0, :N_EXPERTS].set(b_router[i])
            h, info = _router_call(x, mod, row2(g_ffn[layer]), wr_hi, wr_lo, br)
            info_flat = info.reshape(n_tokens, LANES)
            expert_ids = info_flat[:, :TOP_K].astype(jnp.int32)
            row_token, tile_expert, n_valid, p1, p2 = _routing_tables(expert_ids, n_tokens)
            xs = _gather_call(row_token, h.reshape(n_tokens, d))
            ys = _expert_call(tile_expert, n_valid, xs, w_ex_gate, w_ex_up, w_ex_down, i)
            final = layer == depth - 1
            x = _combine_call(p1, p2, x.reshape(n_tokens, d), info_flat, mod[:, 5:6, :], row2(g_final), ys, s,
                              final_norm=final).reshape(b, s, d)
    return x
```

```python
import functools
import math

import jax
import jax.numpy as jnp
from jax import lax
from jax.experimental import pallas as pl
from jax.experimental.pallas import tpu as pltpu

F32 = jnp.float32
BF16 = jnp.bfloat16

EPS = 1e-6
N_MOD = 6
CHUNK = 64
MLA_HEADS = 8
MLA_NOPE = 64
MLA_ROPE = 32
MLA_V = 64
Q_LORA = 256
KV_LORA = 128
ROPE_BASE = 10000.0
SB_HEADS = 8
SB_HEAD_DIM = 64
N_EXPERTS = 8
TOP_K = 2

LANES = 128
HALF_ROPE = MLA_ROPE // 2
MLA_QK_DIM = MLA_NOPE + MLA_ROPE
LOG2E = 1.4426950408889634
NEG_BIG = -1e30

ROW_TILE = 512
ATT_TILE = 256
EXPERT_ROW_TILE = 1024
EXPERT_FF_TILE = 512
GATHER_TILE = 1024
FFN_CHUNK = 512


def _split_bf16(v):
    hi = v.astype(BF16)
    lo = (v - hi.astype(F32)).astype(BF16)
    return hi, lo


def _dot(a, b):
    return jnp.dot(a, b, preferred_element_type=F32)


def _rms_rows(v):
    return v * lax.rsqrt(jnp.mean(v * v, axis=-1, keepdims=True) + EPS)


def _modulated_norm(x, gain, scale, shift):
    return _rms_rows(x) * (gain * (1.0 + scale)) + shift


def _ada_kernel(c_ref, w_ref, b_ref, o_ref):
    c = c_ref[...]
    act = c * (1.0 / (1.0 + jnp.exp(-c)))
    a_hi, a_lo = _split_bf16(act)
    w_hi, w_lo = _split_bf16(w_ref[0])
    o_ref[0] = _dot(a_hi, w_hi) + _dot(a_lo, w_hi) + _dot(a_hi, w_lo) + b_ref[0]


def _ada_call(c_pad, w_ada, b_ada):
    depth, d, n = w_ada.shape
    tn = n // 4
    return pl.pallas_call(
        _ada_kernel,
        grid=(depth, n // tn),
        in_specs=[
            pl.BlockSpec((8, d), lambda l, j: (0, 0)),
            pl.BlockSpec((1, d, tn), lambda l, j: (l, 0, j)),
            pl.BlockSpec((1, 1, tn), lambda l, j: (l, 0, j)),
        ],
        out_specs=pl.BlockSpec((1, 8, tn), lambda l, j: (l, 0, j)),
        out_shape=jax.ShapeDtypeStruct((depth, 8, n), F32),
        name="ada_mod",
    )(c_pad, w_ada, b_ada.reshape(depth, 1, n))


def _rope_table_kernel(pos_ref, freq_ref, cos_ref, sin_ref):
    ang = pos_ref[0] * freq_ref[...]
    lane = lax.broadcasted_iota(jnp.int32, ang.shape, 1)
    in_rope = (lane >= MLA_NOPE) & (lane < MLA_QK_DIM)
    first_half = lane < MLA_NOPE + HALF_ROPE
    cos_ref[0] = jnp.where(in_rope, jnp.cos(ang), jnp.where(lane < MLA_NOPE, 1.0, 0.0))
    s = jnp.sin(ang)
    sin_ref[0] = jnp.where(in_rope, jnp.where(first_half, -s, s), 0.0)


def _rope_tables(pos_lanes, freq_lanes):
    b, s, _ = pos_lanes.shape
    tm = ROW_TILE
    spec = pl.BlockSpec((1, tm, LANES), lambda bi, i: (bi, i, 0))
    return pl.pallas_call(
        _rope_table_kernel,
        grid=(b, s // tm),
        in_specs=[spec, pl.BlockSpec((1, LANES), lambda bi, i: (0, 0))],
        out_specs=[spec, spec],
        out_shape=[jax.ShapeDtypeStruct((b, s, LANES), F32)] * 2,
        name="rope_tables",
    )(pos_lanes, freq_lanes)


IN_OFF_CQ = 0
IN_OFF_CKV = Q_LORA
IN_OFF_KR = IN_OFF_CKV + KV_LORA
IN_OFF_KRS = IN_OFF_KR + LANES
IN_OFF_QSB = IN_OFF_KRS + LANES
SB_WIDTH = SB_HEADS * SB_HEAD_DIM
MLA_WIDTH = MLA_HEADS * MLA_V
IN_OFF_KSB = IN_OFF_QSB + SB_WIDTH
IN_OFF_VSB = IN_OFF_KSB + SB_WIDTH
IN_COLS_PADDED = IN_OFF_VSB + SB_WIDTH
MLA_PAD_WIDTH = MLA_HEADS * LANES


def _proj_kernel(x_ref, mod_ref, gmix_ref, cos_ref, sin_ref, win_ref, gq_ref, wuq_ref, gkv_ref, wukv_ref,
                 qm_ref, km_ref, vtm_ref, qs_ref, ks_ref, vts_ref):
    x = x_ref[0]
    mod = mod_ref[0]
    h = _modulated_norm(x, gmix_ref[...], mod[1:2], mod[0:1])
    proj = _dot(h.astype(BF16), win_ref[...])
    cos = cos_ref[0]
    sin = sin_ref[0]
    n_kv = vtm_ref.shape[1]
    tk = vtm_ref.shape[3]

    cqn = _rms_rows(proj[:, IN_OFF_CQ:IN_OFF_CQ + Q_LORA]) * gq_ref[...]
    q2 = _dot(cqn.astype(BF16), wuq_ref[...])
    q_scale = MLA_QK_DIM ** -0.5 * LOG2E
    for hd in range(MLA_HEADS):
        lo = hd * LANES
        q = q2[:, lo:lo + LANES] * cos + q2[:, MLA_PAD_WIDTH + lo:MLA_PAD_WIDTH + lo + LANES] * sin
        qm_ref[0, :, lo:lo + LANES] = (q * q_scale).astype(qm_ref.dtype)

    ckvn = _rms_rows(proj[:, IN_OFF_CKV:IN_OFF_CKV + KV_LORA]) * gkv_ref[...]
    kv2 = _dot(ckvn.astype(BF16), wukv_ref[...])
    k_rope = proj[:, IN_OFF_KR:IN_OFF_KR + LANES] * cos + proj[:, IN_OFF_KRS:IN_OFF_KRS + LANES] * sin
    for hd in range(MLA_HEADS):
        lo = hd * LANES
        km_ref[0, :, lo:lo + LANES] = (kv2[:, lo:lo + LANES] + k_rope).astype(km_ref.dtype)
    v_t = kv2[:, MLA_PAD_WIDTH:].T
    for c in range(n_kv):
        vtm_ref[0, c] = v_t[:, c * tk:(c + 1) * tk].astype(vtm_ref.dtype)

    qs_ref[0] = proj[:, IN_OFF_QSB:IN_OFF_QSB + SB_WIDTH].astype(qs_ref.dtype)
    ks_ref[0] = proj[:, IN_OFF_KSB:IN_OFF_KSB + SB_WIDTH].astype(ks_ref.dtype)
    vs_t = proj[:, IN_OFF_VSB:IN_OFF_VSB + SB_WIDTH].T
    for c in range(n_kv):
        vts_ref[0, c] = vs_t[:, c * tk:(c + 1) * tk].astype(vts_ref.dtype)


def _proj_call(x, mod, gmix, cos_t, sin_t, win, gq, wuq, gkv, wukv):
    b, s, d = x.shape
    tm = ROW_TILE
    tk = ATT_TILE
    n_kv = tm // tk
    row = lambda w: pl.BlockSpec((1, tm, w), lambda bi, i: (bi, i, 0))
    full = lambda a: pl.BlockSpec(a.shape, lambda bi, i: (0,) * a.ndim)
    vt_spec = lambda w: pl.BlockSpec((1, n_kv, w, tk), lambda bi, i: (bi, i, 0, 0))
    return pl.pallas_call(
        _proj_kernel,
        grid=(b, s // tm),
        in_specs=[row(d), pl.BlockSpec((1, N_MOD, d), lambda bi, i: (bi, 0, 0)), full(gmix), row(LANES), row(LANES),
                  full(win), full(gq), full(wuq), full(gkv), full(wukv)],
        out_specs=[row(MLA_PAD_WIDTH), row(MLA_PAD_WIDTH), vt_spec(MLA_WIDTH),
                   row(SB_WIDTH), row(SB_WIDTH), vt_spec(SB_WIDTH)],
        out_shape=[
            jax.ShapeDtypeStruct((b, s, MLA_PAD_WIDTH), BF16),
            jax.ShapeDtypeStruct((b, s, MLA_PAD_WIDTH), BF16),
            jax.ShapeDtypeStruct((b, s // tk, MLA_WIDTH, tk), BF16),
            jax.ShapeDtypeStruct((b, s, SB_WIDTH), BF16),
            jax.ShapeDtypeStruct((b, s, SB_WIDTH), BF16),
            jax.ShapeDtypeStruct((b, s // tk, SB_WIDTH, tk), BF16),
        ],
        name="in_proj",
    )(x, mod, gmix, cos_t, sin_t, win, gq, wuq, gkv, wukv)


def _resident(shape, index_map):
    return pl.BlockSpec(shape, index_map, pipeline_mode=pl.Buffered(1))


def _mla_kernel(q_ref, k_ref, vt_ref, o_ref, qt_scr, m_scr, l_scr, acc_scr):
    tq = q_ref.shape[1]
    tk = vt_ref.shape[3]
    i = pl.program_id(1)
    for hd in range(MLA_HEADS):
        qt_scr[hd] = q_ref[0, :, hd * LANES:(hd + 1) * LANES].astype(F32).T.astype(BF16)
    m_scr[...] = jnp.full(m_scr.shape, NEG_BIG, F32)
    l_scr[...] = jnp.zeros(l_scr.shape, F32)
    acc_scr[...] = jnp.zeros(acc_scr.shape, F32)

    def step(j, masked):
        start = pl.multiple_of(j * tk, tk)
        scores = [_dot(k_ref[0, pl.ds(start, tk), hd * LANES:(hd + 1) * LANES], qt_scr[hd])
                  for hd in range(MLA_HEADS)]
        probs = []
        alphas = []
        for hd in range(MLA_HEADS):
            s = scores[hd]
            if masked:
                key_chunk = lax.broadcasted_iota(jnp.int32, s.shape, 0) // CHUNK
                qry_chunk = lax.broadcasted_iota(jnp.int32, s.shape, 1) // CHUNK
                s = jnp.where(key_chunk <= qry_chunk, s, NEG_BIG)
            m = m_scr[hd]
            m_new = jnp.maximum(m, jnp.max(s, axis=0, keepdims=True))
            alpha = jnp.exp2(m - m_new)
            p = jnp.exp2(s - m_new)
            m_scr[hd] = m_new
            l_scr[hd] = alpha * l_scr[hd] + jnp.sum(p, axis=0, keepdims=True)
            probs.append(p.astype(BF16))
            alphas.append(alpha)
        for hd in range(MLA_HEADS):
            pv = _dot(vt_ref[0, j, hd * MLA_V:(hd + 1) * MLA_V, :], probs[hd])
            acc_scr[hd] = alphas[hd] * acc_scr[hd] + pv

    def full_step(j, carry):
        step(j, masked=False)
        return carry

    lax.fori_loop(0, i, full_step, 0)
    step(i, masked=True)
    for hd in range(MLA_HEADS):
        o_ref[0, hd * MLA_V:(hd + 1) * MLA_V, :] = acc_scr[hd] * (1.0 / l_scr[hd])


def _mla_call(q, k, vt):
    b, s, _ = q.shape
    n_kv, tk = vt.shape[1], vt.shape[3]
    tq = tk
    return pl.pallas_call(
        _mla_kernel,
        grid=(b, s // tq),
        in_specs=[
            pl.BlockSpec((1, tq, MLA_PAD_WIDTH), lambda bi, i: (bi, i, 0)),
            _resident((1, s, MLA_PAD_WIDTH), lambda bi, i: (bi, 0, 0)),
            _resident((1, n_kv, MLA_WIDTH, tk), lambda bi, i: (bi, 0, 0, 0)),
        ],
        out_specs=pl.BlockSpec((1, MLA_WIDTH, tq), lambda bi, i: (bi, 0, i)),
        out_shape=jax.ShapeDtypeStruct((b, MLA_WIDTH, s), F32),
        scratch_shapes=[pltpu.VMEM((MLA_HEADS, LANES, tq), BF16), pltpu.VMEM((MLA_HEADS, 1, tq), F32),
                        pltpu.VMEM((MLA_HEADS, 1, tq), F32), pltpu.VMEM((MLA_HEADS, MLA_V, tq), F32)],
        name="mla_attention",
    )(q, k, vt)


F32_EXP_ZERO_BELOW = -105.0


def _sb_kernel(q_ref, k_ref, vt_ref, o_ref, qt_scr, later_scr, run_scr, acc_scr):
    tq = q_ref.shape[1]
    tk = vt_ref.shape[3]
    i = pl.program_id(1)
    for pair in range(SB_HEADS // 2):
        q_t = q_ref[0, :, pair * LANES:(pair + 1) * LANES].astype(F32).T
        head_row = lax.broadcasted_iota(jnp.int32, q_t.shape, 0) // SB_HEAD_DIM
        for sub in range(2):
            qt_scr[2 * pair + sub] = jnp.where(head_row == sub, q_t, 0.0).astype(BF16)
    r = lax.broadcasted_iota(jnp.int32, (tk, tk), 0)
    c = lax.broadcasted_iota(jnp.int32, (tk, tk), 1)
    later_scr[...] = jnp.where(c > r, 1.0, 0.0).astype(BF16)
    run_scr[...] = jnp.zeros(run_scr.shape, F32)
    acc_scr[...] = jnp.zeros(acc_scr.shape, F32)

    def step(j, masked):
        start = pl.multiple_of(j * tk, tk)
        zs = [_dot(k_ref[0, pl.ds(start, tk), (hd // 2) * LANES:(hd // 2 + 1) * LANES], qt_scr[hd])
              for hd in range(SB_HEADS)]
        if masked:
            strict = lax.broadcasted_iota(jnp.int32, (tk, tq), 0) < lax.broadcasted_iota(jnp.int32, (tk, tq), 1)
        log_betas, keep_sums, laters = [], [], []
        for hd in range(SB_HEADS):
            z = zs[hd]
            soft = jnp.log(1.0 + jnp.exp(-jnp.abs(z)))
            log_keep = -jnp.maximum(z, 0.0) - soft
            log_betas.append(jnp.minimum(z, 0.0) - soft)
            if masked:
                log_keep = jnp.where(strict, log_keep, 0.0)
            keep_hi, keep_lo = _split_bf16(log_keep)
            laters.append(_dot(later_scr[...], keep_hi) + _dot(later_scr[...], keep_lo))
            keep_sums.append(jnp.sum(log_keep, axis=0, keepdims=True))
        weights = []
        for hd in range(SB_HEADS):
            run = run_scr[hd]
            a = jnp.exp(log_betas[hd] + laters[hd] + run)
            if masked:
                a = jnp.where(strict, a, 0.0)
            weights.append(a.astype(BF16))
            run_scr[hd] = run + keep_sums[hd]
        for hd in range(SB_HEADS):
            acc_scr[hd] += _dot(vt_ref[0, j, hd * SB_HEAD_DIM:(hd + 1) * SB_HEAD_DIM, :], weights[hd])

    def any_weight_left():
        return (jnp.max(run_scr[...]) > F32_EXP_ZERO_BELOW).astype(jnp.int32)

    step(i, masked=True)

    def cond(carry):
        j, alive = carry
        return (j >= 0) & (alive > 0)

    def body(carry):
        j, _ = carry
        step(j, masked=False)
        return j - 1, any_weight_left()

    lax.while_loop(cond, body, (i - 1, any_weight_left()))
    for hd in range(SB_HEADS):
        o_ref[0, hd * SB_HEAD_DIM:(hd + 1) * SB_HEAD_DIM, :] = acc_scr[hd]


def _sb_call(q, k, vt):
    b, s, _ = q.shape
    n_kv, tk = vt.shape[1], vt.shape[3]
    tq = tk
    return pl.pallas_call(
        _sb_kernel,
        grid=(b, s // tq),
        in_specs=[
            pl.BlockSpec((1, tq, SB_WIDTH), lambda bi, i: (bi, i, 0)),
            _resident((1, s, SB_WIDTH), lambda bi, i: (bi, 0, 0)),
            _resident((1, n_kv, SB_WIDTH, tk), lambda bi, i: (bi, 0, 0, 0)),
        ],
        out_specs=pl.BlockSpec((1, SB_WIDTH, tq), lambda bi, i: (bi, 0, i)),
        out_shape=jax.ShapeDtypeStruct((b, SB_WIDTH, s), F32),
        scratch_shapes=[pltpu.VMEM((SB_HEADS, LANES, tq), BF16), pltpu.VMEM((tk, tk), BF16),
                        pltpu.VMEM((SB_HEADS, 1, tq), F32), pltpu.VMEM((SB_HEADS, SB_HEAD_DIM, tq), F32)],
        name="sb_attention",
    )(q, k, vt)


def _merge_kernel(x_ref, om_ref, os_ref, mod_ref, gm_ref, gs_ref, wout_ref, o_ref):
    def group(o_t, gain):
        inv = lax.rsqrt(jnp.mean(o_t * o_t, axis=0, keepdims=True) + EPS)
        return ((o_t * inv).T * gain).astype(BF16)

    a = group(om_ref[0], gm_ref[...])
    bb = group(os_ref[0], gs_ref[...])
    mixed = _dot(a, wout_ref[:MLA_WIDTH, :]) + _dot(bb, wout_ref[MLA_WIDTH:, :])
    o_ref[0] = x_ref[0] + mod_ref[0][2:3] * mixed


def _merge_call(x, o_mla, o_sb, mod, g_mla, g_sb, wout):
    b, s, d = x.shape
    tm = ROW_TILE
    row = pl.BlockSpec((1, tm, d), lambda bi, i: (bi, i, 0))
    col = lambda w: pl.BlockSpec((1, w, tm), lambda bi, i: (bi, 0, i))
    full = lambda a: pl.BlockSpec(a.shape, lambda bi, i: (0,) * a.ndim)
    return pl.pallas_call(
        _merge_kernel,
        grid=(b, s // tm),
        in_specs=[row, col(MLA_WIDTH), col(SB_WIDTH), pl.BlockSpec((1, N_MOD, d), lambda bi, i: (bi, 0, 0)),
                  full(g_mla), full(g_sb), full(wout)],
        out_specs=row,
        out_shape=jax.ShapeDtypeStruct((b, s, d), F32),
        name="merge_out_proj",
    )(x, o_mla, o_sb, mod, g_mla, g_sb, wout)


def _silu(v):
    return v * (1.0 / (1.0 + jnp.exp(-v)))


def _ffn_kernel(x_ref, mod_ref, g_ref, wg_ref, wu_ref, wd_ref, o_ref):
    x = x_ref[0]
    mod = mod_ref[0]
    hb = _modulated_norm(x, g_ref[...], mod[4:5], mod[3:4]).astype(BF16)
    acc = jnp.zeros(x.shape, F32)
    for c in range(wg_ref.shape[1] // FFN_CHUNK):
        lo = c * FFN_CHUNK
        gate = _dot(hb, wg_ref[:, lo:lo + FFN_CHUNK])
        up = _dot(hb, wu_ref[:, lo:lo + FFN_CHUNK])
        acc = acc + _dot((_silu(gate) * up).astype(BF16), wd_ref[lo:lo + FFN_CHUNK, :])
    o_ref[0] = x + mod[5:6] * acc


def _ffn_call(x, mod, g, wg, wu, wd):
    b, s, d = x.shape
    tm = ROW_TILE
    row = pl.BlockSpec((1, tm, d), lambda bi, i: (bi, i, 0))
    full = lambda a: pl.BlockSpec(a.shape, lambda bi, i: (0,) * a.ndim)
    return pl.pallas_call(
        _ffn_kernel,
        grid=(b, s // tm),
        in_specs=[row, pl.BlockSpec((1, N_MOD, d), lambda bi, i: (bi, 0, 0)), full(g), full(wg), full(wu), full(wd)],
        out_specs=row,
        out_shape=jax.ShapeDtypeStruct((b, s, d), F32),
        name="dense_ffn",
    )(x, mod, g, wg, wu, wd)


def _router_kernel(x_ref, mod_ref, g_ref, wr_hi_ref, wr_lo_ref, br_ref, h_ref, info_ref):
    x = x_ref[0]
    mod = mod_ref[0]
    h = _modulated_norm(x, g_ref[...], mod[4:5], mod[3:4])
    h_ref[0] = h
    h_hi, h_lo = _split_bf16(h)
    w_hi = wr_hi_ref[...]
    logits = _dot(h_hi, w_hi) + _dot(h_lo, w_hi) + _dot(h_hi, wr_lo_ref[...]) + br_ref[...]
    lane = lax.broadcasted_iota(jnp.int32, logits.shape, 1)
    logits = jnp.where(lane < N_EXPERTS, logits, NEG_BIG)
    m1 = jnp.max(logits, axis=-1, keepdims=True)
    i1 = jnp.min(jnp.where(logits == m1, lane, LANES), axis=-1, keepdims=True)
    rest = jnp.where(lane == i1, NEG_BIG, logits)
    m2 = jnp.max(rest, axis=-1, keepdims=True)
    i2 = jnp.min(jnp.where(rest == m2, lane, LANES), axis=-1, keepdims=True)
    e = jnp.exp(m2 - m1)
    w1 = 1.0 / (1.0 + e)
    w2 = e * w1
    info = jnp.where(lane == 0, i1.astype(F32),
                     jnp.where(lane == 1, i2.astype(F32), jnp.where(lane == 2, w1, jnp.where(lane == 3, w2, 0.0))))
    info_ref[0] = info


def _router_call(x, mod, g, wr_hi, wr_lo, br):
    b, s, d = x.shape
    tm = ROW_TILE
    row = pl.BlockSpec((1, tm, d), lambda bi, i: (bi, i, 0))
    full = lambda a: pl.BlockSpec(a.shape, lambda bi, i: (0,) * a.ndim)
    return pl.pallas_call(
        _router_kernel,
        grid=(b, s // tm),
        in_specs=[row, pl.BlockSpec((1, N_MOD, d), lambda bi, i: (bi, 0, 0)), full(g), full(wr_hi), full(wr_lo), full(br)],
        out_specs=[row, pl.BlockSpec((1, tm, LANES), lambda bi, i: (bi, i, 0))],
        out_shape=[jax.ShapeDtypeStruct((b, s, d), F32), jax.ShapeDtypeStruct((b, s, LANES), F32)],
        name="moe_router",
    )(x, mod, g, wr_hi, wr_lo, br)


def _row_copy(src_hbm, dst_vmem, sem, src_row, dst_row):
    return pltpu.make_async_copy(src_hbm.at[pl.ds(src_row, 1), :], dst_vmem.at[pl.ds(dst_row, 1), :], sem)


def _gather_rows(idx_ref, src_hbm, dst_vmem, sem, n_rows):
    def start(r, _):
        _row_copy(src_hbm, dst_vmem, sem, idx_ref[r], r).start()
        return 0

    def wait(r, _):
        _row_copy(src_hbm, dst_vmem, sem, 0, r).wait()
        return 0

    lax.fori_loop(0, n_rows, start, 0)
    lax.fori_loop(0, n_rows, wait, 0)


def _gather_kernel(idx_ref, src_ref, o_ref, sem):
    _gather_rows(idx_ref, src_ref, o_ref, sem, o_ref.shape[0])


def _gather_call(row_token, h_flat):
    n_rows = row_token.shape[0]
    d = h_flat.shape[1]
    tg = GATHER_TILE
    return pl.pallas_call(
        _gather_kernel,
        grid=(n_rows // tg,),
        in_specs=[pl.BlockSpec((tg,), lambda i: (i,), memory_space=pltpu.SMEM), pl.BlockSpec(memory_space=pl.ANY)],
        out_specs=pl.BlockSpec((tg, d), lambda i: (i, 0)),
        out_shape=jax.ShapeDtypeStruct((n_rows, d), h_flat.dtype),
        scratch_shapes=[pltpu.SemaphoreType.DMA(())],
        name="moe_gather",
    )(row_token, h_flat)


def _expert_kernel(tile_expert_ref, n_valid_ref, xs_ref, wg_ref, wu_ref, wd_ref, o_ref):
    i = pl.program_id(0)
    j = pl.program_id(1)

    @pl.when(i < n_valid_ref[0])
    def _():
        xb = xs_ref[...].astype(BF16)
        gate = _dot(xb, wg_ref[0, 0].astype(BF16))
        up = _dot(xb, wu_ref[0, 0].astype(BF16))
        part = _dot((_silu(gate) * up).astype(BF16), wd_ref[0, 0].astype(BF16))

        @pl.when(j == 0)
        def _():
            o_ref[...] = part

        @pl.when(j > 0)
        def _():
            o_ref[...] += part

    @pl.when((i >= n_valid_ref[0]) & (j == 0))
    def _():
        o_ref[...] = jnp.zeros(o_ref.shape, o_ref.dtype)


def _expert_call(tile_expert, n_valid, xs, wg, wu, wd, layer_idx):
    n_rows, d = xs.shape
    tm = EXPERT_ROW_TILE
    tf = EXPERT_FF_TILE
    n_ff = wg.shape[-1]

    def tile(i, te, nv):
        return jnp.minimum(i, nv[0] - 1)

    grid_spec = pltpu.PrefetchScalarGridSpec(
        num_scalar_prefetch=2,
        grid=(n_rows // tm, n_ff // tf),
        in_specs=[
            pl.BlockSpec((tm, d), lambda i, j, te, nv: (tile(i, te, nv), 0)),
            pl.BlockSpec((1, 1, d, tf), lambda i, j, te, nv: (layer_idx, te[tile(i, te, nv)], 0, jnp.where(i < nv[0], j, n_ff // tf - 1))),
            pl.BlockSpec((1, 1, d, tf), lambda i, j, te, nv: (layer_idx, te[tile(i, te, nv)], 0, jnp.where(i < nv[0], j, n_ff // tf - 1))),
            pl.BlockSpec((1, 1, tf, d), lambda i, j, te, nv: (layer_idx, te[tile(i, te, nv)], jnp.where(i < nv[0], j, n_ff // tf - 1), 0)),
        ],
        out_specs=pl.BlockSpec((tm, d), lambda i, j, te, nv: (i, 0)),
    )
    return pl.pallas_call(
        _expert_kernel,
        grid_spec=grid_spec,
        out_shape=jax.ShapeDtypeStruct((n_rows, d), F32),
        name="moe_experts",
    )(tile_expert, n_valid, xs, wg, wu, wd)


def _combine_kernel(p1_ref, p2_ref, x_ref, info_ref, gate_ref, gfin_ref, y_ref, o_ref, buf1, buf2, sem1, sem2, *, final_norm):
    n = x_ref.shape[0]

    def start(r, _):
        _row_copy(y_ref, buf1, sem1, p1_ref[r], r).start()
        _row_copy(y_ref, buf2, sem2, p2_ref[r], r).start()
        return 0

    def wait(r, _):
        _row_copy(y_ref, buf1, sem1, 0, r).wait()
        _row_copy(y_ref, buf2, sem2, 0, r).wait()
        return 0

    lax.fori_loop(0, n, start, 0)
    lax.fori_loop(0, n, wait, 0)
    info = info_ref[...]
    y = info[:, 2:3] * buf1[...] + info[:, 3:4] * buf2[...]
    out = x_ref[...] + gate_ref[0] * y
    if final_norm:
        out = _rms_rows(out) * gfin_ref[...]
    o_ref[...] = out


def _combine_call(p1, p2, x_flat, info_flat, gate_rows, g_final, y_sorted, tokens_per_batch, final_norm):
    t, d = x_flat.shape
    tc = GATHER_TILE
    per_b = tokens_per_batch // tc
    idx = pl.BlockSpec((tc,), lambda i: (i,), memory_space=pltpu.SMEM)
    return pl.pallas_call(
        functools.partial(_combine_kernel, final_norm=final_norm),
        grid=(t // tc,),
        in_specs=[idx, idx, pl.BlockSpec((tc, d), lambda i: (i, 0)), pl.BlockSpec((tc, LANES), lambda i: (i, 0)),
                  pl.BlockSpec((1, 1, d), lambda i: (i // per_b, 0, 0)), pl.BlockSpec((1, d), lambda i: (0, 0)),
                  pl.BlockSpec(memory_space=pl.ANY)],
        out_specs=pl.BlockSpec((tc, d), lambda i: (i, 0)),
        out_shape=jax.ShapeDtypeStruct((t, d), F32),
        scratch_shapes=[pltpu.VMEM((tc, d), F32), pltpu.VMEM((tc, d), F32),
                        pltpu.SemaphoreType.DMA(()), pltpu.SemaphoreType.DMA(())],
        name="moe_combine",
    )(p1, p2, x_flat, info_flat, gate_rows, g_final, y_sorted)


def _routing_tables(expert_ids, n_tokens):
    tm = EXPERT_ROW_TILE
    n_assign = n_tokens * TOP_K
    n_rows = n_assign + N_EXPERTS * tm
    e_flat = expert_ids.reshape(n_assign)
    onehot = (e_flat[:, None] == jnp.arange(N_EXPERTS, dtype=jnp.int32)[None, :]).astype(jnp.int32)
    rank = jnp.sum((jnp.cumsum(onehot, axis=0) - onehot) * onehot, axis=1)
    counts = jnp.sum(onehot, axis=0)
    padded = ((counts + tm - 1) // tm) * tm
    ends = jnp.cumsum(padded)
    starts = ends - padded
    pos = (starts[e_flat] + rank).astype(jnp.int32)
    row_token = jnp.zeros((n_rows,), jnp.int32).at[pos].set(jnp.arange(n_assign, dtype=jnp.int32) // TOP_K)
    n_valid = (ends[-1] // tm).astype(jnp.int32).reshape(1)
    tile_start = jnp.arange(n_rows // tm, dtype=jnp.int32) * tm
    tile_expert = jnp.minimum(jnp.sum((ends[None, :] <= tile_start[:, None]).astype(jnp.int32), axis=1), N_EXPERTS - 1)
    pos2 = pos.reshape(n_tokens, TOP_K)
    return row_token, tile_expert, n_valid, pos2[:, 0], pos2[:, 1]


def _prep_w_in(w_in):
    d = w_in.shape[0]
    cq, ckv, kr, qsb, ksb, vsb = jnp.split(
        w_in, (Q_LORA, Q_LORA + KV_LORA, Q_LORA + KV_LORA + MLA_ROPE,
               Q_LORA + KV_LORA + MLA_ROPE + SB_WIDTH, Q_LORA + KV_LORA + MLA_ROPE + 2 * SB_WIDTH), axis=1)
    z = lambda n: jnp.zeros((d, n), w_in.dtype)
    kr_blk = jnp.concatenate([z(MLA_NOPE), kr, z(LANES - MLA_QK_DIM)], axis=1)
    kr_swap = jnp.concatenate([z(MLA_NOPE), kr[:, HALF_ROPE:], kr[:, :HALF_ROPE], z(LANES - MLA_QK_DIM)], axis=1)
    return jnp.concatenate([cq, ckv, kr_blk, kr_swap, qsb * SB_HEAD_DIM ** -0.5, ksb, vsb], axis=1).astype(BF16)


def _prep_w_uq(w_uq):
    r = w_uq.shape[0]
    w = w_uq.reshape(r, MLA_HEADS, MLA_QK_DIM)
    nope, x1, x2 = w[..., :MLA_NOPE], w[..., MLA_NOPE:MLA_NOPE + HALF_ROPE], w[..., MLA_NOPE + HALF_ROPE:]
    pad = jnp.zeros((r, MLA_HEADS, LANES - MLA_QK_DIM), w.dtype)
    plain = jnp.concatenate([nope, x1, x2, pad], axis=-1).reshape(r, MLA_PAD_WIDTH)
    swapped = jnp.concatenate([jnp.zeros_like(nope), x2, x1, pad], axis=-1).reshape(r, MLA_PAD_WIDTH)
    return jnp.concatenate([plain, swapped], axis=1).astype(BF16)


def _prep_w_ukv(w_ukv):
    r = w_ukv.shape[0]
    w = w_ukv.reshape(r, MLA_HEADS, MLA_NOPE + MLA_V)
    k_nope, v = w[..., :MLA_NOPE], w[..., MLA_NOPE:]
    k_pad = jnp.concatenate([k_nope, jnp.zeros((r, MLA_HEADS, LANES - MLA_NOPE), w.dtype)], axis=-1)
    return jnp.concatenate([k_pad.reshape(r, MLA_PAD_WIDTH), v.reshape(r, MLA_WIDTH)], axis=1).astype(BF16)


def kernel(x, c, positions, w_ada, b_ada, g_mix, g_ffn, w_in, g_q_lat, w_uq, g_kv_lat, w_ukv, g_grp_mla, g_grp_sb, w_out, w_ff_gate, w_ff_up, w_ff_down, w_router, b_router, w_ex_gate, w_ex_up, w_ex_down, g_final):
    b, s, d = x.shape
    depth = w_ada.shape[0]
    n_tokens = b * s
    assert depth % 2 == 0, "the final norm is fused into the routed mixer of the last layer"

    c_pad = jnp.zeros((8, d), F32).at[:b].set(c)
    mod_all = _ada_call(c_pad, w_ada, b_ada)[:, :b].reshape(depth, b, N_MOD, d)

    lane = jnp.arange(LANES)
    freq = ROPE_BASE ** (-((lane - MLA_NOPE) % HALF_ROPE).astype(F32) / HALF_ROPE)
    pos_lanes = jnp.broadcast_to(positions.astype(F32)[..., None], (b, s, LANES))
    cos_t, sin_t = _rope_tables(pos_lanes, freq.reshape(1, LANES))

    row2 = lambda v: v.reshape(1, -1)
    for layer in range(depth):
        mod = mod_all[layer]
        qm, km, vtm, qs, ks, vts = _proj_call(
            x, mod, row2(g_mix[layer]), cos_t, sin_t, _prep_w_in(w_in[layer]), row2(g_q_lat[layer]),
            _prep_w_uq(w_uq[layer]), row2(g_kv_lat[layer]), _prep_w_ukv(w_ukv[layer]))
        o_mla = _mla_call(qm, km, vtm)
        o_sb = _sb_call(qs, ks, vts)
        x = _merge_call(x, o_mla, o_sb, mod, row2(g_grp_mla[layer]), row2(g_grp_sb[layer]), w_out[layer].astype(BF16))

        i = layer // 2
        if layer % 2 == 0:
            x = _ffn_call(x, mod, row2(g_ffn[layer]), w_ff_gate[i].astype(BF16), w_ff_up[i].astype(BF16),
                          w_ff_down[i].astype(BF16))
        else:
            wr = jnp.zeros((d, LANES), F32).at[:, :N_EXPERTS].set(w_router[i])
            wr_hi = wr.astype(BF16)
            wr_lo = (wr - wr_hi.astype(F32)).astype(BF16)
            br = jnp.zeros((1, LANES), F32).at[0, :N_EXPERTS].set(b_router[i])
            h, info = _router_call(x, mod, row2(g_ffn[layer]), wr_hi, wr_lo, br)
            info_flat = info.reshape(n_tokens, LANES)
            expert_ids = info_flat[:, :TOP_K].astype(jnp.int32)
            row_token, tile_expert, n_valid, p1, p2 = _routing_tables(expert_ids, n_tokens)
            xs = _gather_call(row_token, h.reshape(n_tokens, d))
            ys = _expert_call(tile_expert, n_valid, xs, w_ex_gate, w_ex_up, w_ex_down, i)
            final = layer == depth - 1
            x = _combine_call(p1, p2, x.reshape(n_tokens, d), info_flat, mod[:, 5:6, :], row2(g_final), ys, s,
                              final_norm=final).reshape(b, s, d)
    return x
```

```python
import functools

import jax
import jax.numpy as jnp
from jax import lax
from jax.experimental import pallas as pl
from jax.experimental.pallas import tpu as pltpu

F32 = jnp.float32
BF16 = jnp.bfloat16

EPS = 1e-6
N_MOD = 6
CHUNK = 64
MLA_HEADS = 8
MLA_NOPE = 64
MLA_ROPE = 32
MLA_V = 64
Q_LORA = 256
KV_LORA = 128
ROPE_BASE = 10000.0
SB_HEADS = 8
SB_HEAD_DIM = 64
N_EXPERTS = 8
TOP_K = 2

LANES = 128
HALF_ROPE = MLA_ROPE // 2
MLA_QK_DIM = MLA_NOPE + MLA_ROPE
LOG2E = 1.4426950408889634
NEG_BIG = -1e30

ROW_TILE = 512
ATT_TILE = 256
EXPERT_ROW_TILE = 1024
EXPERT_FF_TILE = 512
GATHER_TILE = 1024
FFN_CHUNK = 512


def _split_bf16(v):
    hi = v.astype(BF16)
    lo = (v - hi.astype(F32)).astype(BF16)
    return hi, lo


def _dot(a, b):
    return jnp.dot(a, b, preferred_element_type=F32)


def _rms_rows(v):
    return v * lax.rsqrt(jnp.mean(v * v, axis=-1, keepdims=True) + EPS)


def _modulated_norm(x, gain, scale, shift):
    return _rms_rows(x) * (gain * (1.0 + scale)) + shift


def _ada_kernel(c_ref, w_ref, b_ref, o_ref):
    c = c_ref[...]
    act = c * (1.0 / (1.0 + jnp.exp(-c)))
    a_hi, a_lo = _split_bf16(act)
    w_hi, w_lo = _split_bf16(w_ref[0])
    o_ref[0] = _dot(a_hi, w_hi) + _dot(a_lo, w_hi) + _dot(a_hi, w_lo) + b_ref[0]


def _ada_call(c_pad, w_ada, b_ada):
    depth, d, n = w_ada.shape
    tn = n // 4
    return pl.pallas_call(
        _ada_kernel,
        grid=(depth, n // tn),
        in_specs=[
            pl.BlockSpec((8, d), lambda l, j: (0, 0)),
            pl.BlockSpec((1, d, tn), lambda l, j: (l, 0, j)),
            pl.BlockSpec((1, 1, tn), lambda l, j: (l, 0, j)),
        ],
        out_specs=pl.BlockSpec((1, 8, tn), lambda l, j: (l, 0, j)),
        out_shape=jax.ShapeDtypeStruct((depth, 8, n), F32),
        name="ada_mod",
    )(c_pad, w_ada, b_ada.reshape(depth, 1, n))


def _rope_table_kernel(pos_ref, freq_ref, cos_ref, sin_ref):
    ang = pos_ref[0] * freq_ref[...]
    lane = lax.broadcasted_iota(jnp.int32, ang.shape, 1)
    in_rope = (lane >= MLA_NOPE) & (lane < MLA_QK_DIM)
    first_half = lane < MLA_NOPE + HALF_ROPE
    cos_ref[0] = jnp.where(in_rope, jnp.cos(ang), jnp.where(lane < MLA_NOPE, 1.0, 0.0))
    s = jnp.sin(ang)
    sin_ref[0] = jnp.where(in_rope, jnp.where(first_half, -s, s), 0.0)


def _rope_tables(pos_lanes, freq_lanes):
    b, s, _ = pos_lanes.shape
    tm = ROW_TILE
    spec = pl.BlockSpec((1, tm, LANES), lambda bi, i: (bi, i, 0))
    return pl.pallas_call(
        _rope_table_kernel,
        grid=(b, s // tm),
        in_specs=[spec, pl.BlockSpec((1, LANES), lambda bi, i: (0, 0))],
        out_specs=[spec, spec],
        out_shape=[jax.ShapeDtypeStruct((b, s, LANES), F32)] * 2,
        name="rope_tables",
    )(pos_lanes, freq_lanes)


IN_OFF_CQ = 0
IN_OFF_CKV = Q_LORA
IN_OFF_KR = IN_OFF_CKV + KV_LORA
IN_OFF_KRS = IN_OFF_KR + LANES
IN_OFF_QSB = IN_OFF_KRS + LANES
SB_WIDTH = SB_HEADS * SB_HEAD_DIM
MLA_WIDTH = MLA_HEADS * MLA_V
IN_OFF_KSB = IN_OFF_QSB + SB_WIDTH
IN_OFF_VSB = IN_OFF_KSB + SB_WIDTH
IN_COLS_PADDED = IN_OFF_VSB + SB_WIDTH
MLA_PAD_WIDTH = MLA_HEADS * LANES
MLA_V_EXT = MLA_V + 16


def _proj_kernel(x_ref, mod_ref, gmix_ref, cos_ref, sin_ref, win_ref, gq_ref, wuq_ref, gkv_ref, wukv_ref,
                 qm_ref, km_ref, vtm_ref, qs_ref, ks_ref, vts_ref):
    x = x_ref[0]
    mod = mod_ref[0]
    h = _modulated_norm(x, gmix_ref[...], mod[1:2], mod[0:1])
    proj = _dot(h.astype(BF16), win_ref[...])
    cos = cos_ref[0]
    sin = sin_ref[0]
    n_kv = vtm_ref.shape[1]
    tk = vtm_ref.shape[3]

    cqn = _rms_rows(proj[:, IN_OFF_CQ:IN_OFF_CQ + Q_LORA]) * gq_ref[...]
    q2 = _dot(cqn.astype(BF16), wuq_ref[...])
    q_scale = MLA_QK_DIM ** -0.5 * LOG2E
    for hd in range(MLA_HEADS):
        lo = hd * LANES
        q = q2[:, lo:lo + LANES] * cos + q2[:, MLA_PAD_WIDTH + lo:MLA_PAD_WIDTH + lo + LANES] * sin
        qm_ref[0, :, lo:lo + LANES] = (q * q_scale).astype(qm_ref.dtype)

    ckvn = _rms_rows(proj[:, IN_OFF_CKV:IN_OFF_CKV + KV_LORA]) * gkv_ref[...]
    kv2 = _dot(ckvn.astype(BF16), wukv_ref[...])
    k_rope = proj[:, IN_OFF_KR:IN_OFF_KR + LANES] * cos + proj[:, IN_OFF_KRS:IN_OFF_KRS + LANES] * sin
    for hd in range(MLA_HEADS):
        lo = hd * LANES
        km_ref[0, :, lo:lo + LANES] = (kv2[:, lo:lo + LANES] + k_rope).astype(km_ref.dtype)
    v_t = kv2[:, MLA_PAD_WIDTH:].T
    pad_rows = MLA_V_EXT - MLA_V
    ones_row = jnp.where(lax.broadcasted_iota(jnp.int32, (pad_rows, tk), 0) == 0, 1.0, 0.0).astype(vtm_ref.dtype)
    for c in range(n_kv):
        for hd in range(MLA_HEADS):
            lo = hd * MLA_V_EXT
            vtm_ref[0, c, lo:lo + MLA_V, :] = v_t[hd * MLA_V:(hd + 1) * MLA_V, c * tk:(c + 1) * tk].astype(vtm_ref.dtype)
            vtm_ref[0, c, lo + MLA_V:lo + MLA_V_EXT, :] = ones_row

    qs_ref[0] = proj[:, IN_OFF_QSB:IN_OFF_QSB + SB_WIDTH].astype(qs_ref.dtype)
    ks_ref[0] = proj[:, IN_OFF_KSB:IN_OFF_KSB + SB_WIDTH].astype(ks_ref.dtype)
    vs_t = proj[:, IN_OFF_VSB:IN_OFF_VSB + SB_WIDTH].T
    for c in range(n_kv):
        vts_ref[0, c] = vs_t[:, c * tk:(c + 1) * tk].astype(vts_ref.dtype)


def _proj_call(x, mod, gmix, cos_t, sin_t, win, gq, wuq, gkv, wukv):
    b, s, d = x.shape
    tm = ROW_TILE
    tk = ATT_TILE
    n_kv = tm // tk
    row = lambda w: pl.BlockSpec((1, tm, w), lambda bi, i: (bi, i, 0))
    full = lambda a: pl.BlockSpec(a.shape, lambda bi, i: (0,) * a.ndim)
    vt_spec = lambda w: pl.BlockSpec((1, n_kv, w, tk), lambda bi, i: (bi, i, 0, 0))
    return pl.pallas_call(
        _proj_kernel,
        grid=(b, s // tm),
        in_specs=[row(d), pl.BlockSpec((1, N_MOD, d), lambda bi, i: (bi, 0, 0)), full(gmix), row(LANES), row(LANES),
                  full(win), full(gq), full(wuq), full(gkv), full(wukv)],
        out_specs=[row(MLA_PAD_WIDTH), row(MLA_PAD_WIDTH), vt_spec(MLA_HEADS * MLA_V_EXT),
                   row(SB_WIDTH), row(SB_WIDTH), vt_spec(SB_WIDTH)],
        out_shape=[
            jax.ShapeDtypeStruct((b, s, MLA_PAD_WIDTH), BF16),
            jax.ShapeDtypeStruct((b, s, MLA_PAD_WIDTH), BF16),
            jax.ShapeDtypeStruct((b, s // tk, MLA_HEADS * MLA_V_EXT, tk), BF16),
            jax.ShapeDtypeStruct((b, s, SB_WIDTH), BF16),
            jax.ShapeDtypeStruct((b, s, SB_WIDTH), BF16),
            jax.ShapeDtypeStruct((b, s // tk, SB_WIDTH, tk), BF16),
        ],
        name="in_proj",
    )(x, mod, gmix, cos_t, sin_t, win, gq, wuq, gkv, wukv)


def _resident(shape, index_map):
    return pl.BlockSpec(shape, index_map, pipeline_mode=pl.Buffered(1))


def _mla_kernel(q_ref, k_ref, vt_ref, o_ref, qt_scr, sc_a, sc_b, m_scr, acc_scr):
    tq = q_ref.shape[1]
    tk = vt_ref.shape[3]
    i = pl.program_id(1)
    for hd in range(MLA_HEADS):
        qt_scr[hd] = q_ref[0, :, hd * LANES:(hd + 1) * LANES].astype(F32).T.astype(BF16)
    m_scr[...] = jnp.full(m_scr.shape, NEG_BIG, F32)
    acc_scr[...] = jnp.zeros(acc_scr.shape, F32)

    def scores_into(j, sc):
        start = pl.multiple_of(j * tk, tk)
        for hd in range(MLA_HEADS):
            sc[hd] = _dot(k_ref[0, pl.ds(start, tk), hd * LANES:(hd + 1) * LANES], qt_scr[hd])

    def softmax_pv(j, sc, masked):
        probs = []
        alphas = []
        for hd in range(MLA_HEADS):
            s = sc[hd]
            if masked:
                key_chunk = lax.broadcasted_iota(jnp.int32, s.shape, 0) // CHUNK
                qry_chunk = lax.broadcasted_iota(jnp.int32, s.shape, 1) // CHUNK
                s = jnp.where(key_chunk <= qry_chunk, s, NEG_BIG)
            m = m_scr[hd]
            m_new = jnp.maximum(m, jnp.max(s, axis=0, keepdims=True))
            m_scr[hd] = m_new
            alphas.append(jnp.exp2(m - m_new))
            probs.append(jnp.exp2(s - m_new).astype(BF16))
        for hd in range(MLA_HEADS):
            pv = _dot(vt_ref[0, j, hd * MLA_V_EXT:(hd + 1) * MLA_V_EXT, :], probs[hd])
            acc_scr[hd] = alphas[hd] * acc_scr[hd] + pv

    def two_blocks(t, carry):
        j = 2 * t
        scores_into(j + 1, sc_b)
        softmax_pv(j, sc_a, masked=False)
        scores_into(j + 2, sc_a)
        softmax_pv(j + 1, sc_b, masked=False)
        return carry

    scores_into(0, sc_a)
    lax.fori_loop(0, i // 2, two_blocks, 0)

    @pl.when(i % 2 == 1)
    def _():
        scores_into(i, sc_b)
        softmax_pv(i - 1, sc_a, masked=False)
        softmax_pv(i, sc_b, masked=True)

    @pl.when(i % 2 == 0)
    def _():
        softmax_pv(i, sc_a, masked=True)

    for hd in range(MLA_HEADS):
        acc = acc_scr[hd]
        o_ref[0, hd * MLA_V:(hd + 1) * MLA_V, :] = acc[:MLA_V] * (1.0 / acc[MLA_V:MLA_V + 1])


def _mla_call(q, k, vt):
    b, s, _ = q.shape
    n_kv, tk = vt.shape[1], vt.shape[3]
    tq = tk
    scores = pltpu.VMEM((MLA_HEADS, tk, tq), F32)
    return pl.pallas_call(
        _mla_kernel,
        grid=(b, s // tq),
        in_specs=[
            pl.BlockSpec((1, tq, MLA_PAD_WIDTH), lambda bi, i: (bi, i, 0)),
            _resident((1, s, MLA_PAD_WIDTH), lambda bi, i: (bi, 0, 0)),
            _resident((1, n_kv, MLA_HEADS * MLA_V_EXT, tk), lambda bi, i: (bi, 0, 0, 0)),
        ],
        out_specs=pl.BlockSpec((1, MLA_WIDTH, tq), lambda bi, i: (bi, 0, i)),
        out_shape=jax.ShapeDtypeStruct((b, MLA_WIDTH, s), F32),
        scratch_shapes=[pltpu.VMEM((MLA_HEADS, LANES, tq), BF16), scores, scores,
                        pltpu.VMEM((MLA_HEADS, 1, tq), F32), pltpu.VMEM((MLA_HEADS, MLA_V_EXT, tq), F32)],
        name="mla_attention",
    )(q, k, vt)


F32_EXP2_ZERO_BELOW = -152.0


def _sb_kernel(q_ref, k_ref, vt_ref, o_ref, qt_scr, later_scr, run_scr, acc_scr):
    tq = q_ref.shape[1]
    tk = vt_ref.shape[3]
    i = pl.program_id(1)
    for pair in range(SB_HEADS // 2):
        q_t = q_ref[0, :, pair * LANES:(pair + 1) * LANES].astype(F32).T * LOG2E
        head_row = lax.broadcasted_iota(jnp.int32, q_t.shape, 0) // SB_HEAD_DIM
        for sub in range(2):
            qt_scr[2 * pair + sub] = jnp.where(head_row == sub, q_t, 0.0).astype(BF16)
    r = lax.broadcasted_iota(jnp.int32, later_scr.shape, 0)
    c = lax.broadcasted_iota(jnp.int32, later_scr.shape, 1) % tk
    later_scr[...] = jnp.where(((r < tk) & (c > r)) | (r == tk), 1.0, 0.0).astype(BF16)
    run_scr[...] = jnp.zeros(run_scr.shape, F32)
    acc_scr[...] = jnp.zeros(acc_scr.shape, F32)

    def step(j, masked):
        start = pl.multiple_of(j * tk, tk)
        zs = [_dot(k_ref[0, pl.ds(start, tk), (hd // 2) * LANES:(hd // 2 + 1) * LANES], qt_scr[hd])
              for hd in range(SB_HEADS)]
        if masked:
            strict = lax.broadcasted_iota(jnp.int32, (tk, tq), 0) < lax.broadcasted_iota(jnp.int32, (tk, tq), 1)
        log_betas, laters = [], []
        for hd in range(SB_HEADS):
            z = zs[hd]
            soft = jnp.log2(1.0 + jnp.exp2(-jnp.abs(z)))
            drop = jnp.maximum(z, 0.0) + soft
            log_betas.append(jnp.minimum(z, 0.0) - soft)
            if masked:
                drop = jnp.where(strict, drop, 0.0)
            drop_hi, drop_lo = _split_bf16(drop)
            laters.append(_dot(later_scr[...], jnp.concatenate([drop_hi, drop_lo], axis=0)))
        weights = []
        for hd in range(SB_HEADS):
            run = run_scr[hd]
            a = jnp.exp2(log_betas[hd] - laters[hd][:tk] - run)
            if masked:
                a = jnp.where(strict, a, 0.0)
            weights.append(a.astype(BF16))
            run_scr[hd] = run + laters[hd][tk:tk + 1]
        for hd in range(SB_HEADS):
            acc_scr[hd] += _dot(vt_ref[0, j, hd * SB_HEAD_DIM:(hd + 1) * SB_HEAD_DIM, :], weights[hd])

    def any_weight_left():
        return (jnp.min(run_scr[...]) < -F32_EXP2_ZERO_BELOW).astype(jnp.int32)

    step(i, masked=True)

    def cond(carry):
        j, alive = carry
        return (j >= 0) & (alive > 0)

    def body(carry):
        j, _ = carry
        step(j, masked=False)
        return j - 1, any_weight_left()

    lax.while_loop(cond, body, (i - 1, any_weight_left()))
    for hd in range(SB_HEADS):
        o_ref[0, hd * SB_HEAD_DIM:(hd + 1) * SB_HEAD_DIM, :] = acc_scr[hd]


def _sb_call(q, k, vt):
    b, s, _ = q.shape
    n_kv, tk = vt.shape[1], vt.shape[3]
    tq = tk
    return pl.pallas_call(
        _sb_kernel,
        grid=(b, s // tq),
        in_specs=[
            pl.BlockSpec((1, tq, SB_WIDTH), lambda bi, i: (bi, i, 0)),
            _resident((1, s, SB_WIDTH), lambda bi, i: (bi, 0, 0)),
            _resident((1, n_kv, SB_WIDTH, tk), lambda bi, i: (bi, 0, 0, 0)),
        ],
        out_specs=pl.BlockSpec((1, SB_WIDTH, tq), lambda bi, i: (bi, 0, i)),
        out_shape=jax.ShapeDtypeStruct((b, SB_WIDTH, s), F32),
        scratch_shapes=[pltpu.VMEM((SB_HEADS, LANES, tq), BF16), pltpu.VMEM((tk + 16, 2 * tk), BF16),
                        pltpu.VMEM((SB_HEADS, 1, tq), F32), pltpu.VMEM((SB_HEADS, SB_HEAD_DIM, tq), F32)],
        name="sb_attention",
    )(q, k, vt)


def _merge_kernel(x_ref, om_ref, os_ref, mod_ref, gm_ref, gs_ref, wout_ref, o_ref):
    def group(o_t, gain):
        inv = lax.rsqrt(jnp.mean(o_t * o_t, axis=0, keepdims=True) + EPS)
        return ((o_t * inv).T * gain).astype(BF16)

    a = group(om_ref[0], gm_ref[...])
    bb = group(os_ref[0], gs_ref[...])
    mixed = _dot(a, wout_ref[:MLA_WIDTH, :]) + _dot(bb, wout_ref[MLA_WIDTH:, :])
    o_ref[0] = x_ref[0] + mod_ref[0][2:3] * mixed


def _merge_call(x, o_mla, o_sb, mod, g_mla, g_sb, wout):
    b, s, d = x.shape
    tm = ROW_TILE
    row = pl.BlockSpec((1, tm, d), lambda bi, i: (bi, i, 0))
    col = lambda w: pl.BlockSpec((1, w, tm), lambda bi, i: (bi, 0, i))
    full = lambda a: pl.BlockSpec(a.shape, lambda bi, i: (0,) * a.ndim)
    return pl.pallas_call(
        _merge_kernel,
        grid=(b, s // tm),
        in_specs=[row, col(MLA_WIDTH), col(SB_WIDTH), pl.BlockSpec((1, N_MOD, d), lambda bi, i: (bi, 0, 0)),
                  full(g_mla), full(g_sb), full(wout)],
        out_specs=row,
        out_shape=jax.ShapeDtypeStruct((b, s, d), F32),
        name="merge_out_proj",
    )(x, o_mla, o_sb, mod, g_mla, g_sb, wout)


def _silu(v):
    return v * (1.0 / (1.0 + jnp.exp(-v)))


def _ffn_kernel(x_ref, mod_ref, g_ref, wg_ref, wu_ref, wd_ref, o_ref):
    x = x_ref[0]
    mod = mod_ref[0]
    hb = _modulated_norm(x, g_ref[...], mod[4:5], mod[3:4]).astype(BF16)
    acc = jnp.zeros(x.shape, F32)
    for c in range(wg_ref.shape[1] // FFN_CHUNK):
        lo = c * FFN_CHUNK
        gate = _dot(hb, wg_ref[:, lo:lo + FFN_CHUNK])
        up = _dot(hb, wu_ref[:, lo:lo + FFN_CHUNK])
        acc = acc + _dot((_silu(gate) * up).astype(BF16), wd_ref[lo:lo + FFN_CHUNK, :])
    o_ref[0] = x + mod[5:6] * acc


def _ffn_call(x, mod, g, wg, wu, wd):
    b, s, d = x.shape
    tm = ROW_TILE
    row = pl.BlockSpec((1, tm, d), lambda bi, i: (bi, i, 0))
    full = lambda a: pl.BlockSpec(a.shape, lambda bi, i: (0,) * a.ndim)
    return pl.pallas_call(
        _ffn_kernel,
        grid=(b, s // tm),
        in_specs=[row, pl.BlockSpec((1, N_MOD, d), lambda bi, i: (bi, 0, 0)), full(g), full(wg), full(wu), full(wd)],
        out_specs=row,
        out_shape=jax.ShapeDtypeStruct((b, s, d), F32),
        name="dense_ffn",
    )(x, mod, g, wg, wu, wd)


def _router_kernel(x_ref, mod_ref, g_ref, wr_hi_ref, wr_lo_ref, br_ref, h_ref, info_ref):
    x = x_ref[0]
    mod = mod_ref[0]
    h = _modulated_norm(x, g_ref[...], mod[4:5], mod[3:4])
    h_ref[0] = h
    h_hi, h_lo = _split_bf16(h)
    w_hi = wr_hi_ref[...]
    logits = _dot(h_hi, w_hi) + _dot(h_lo, w_hi) + _dot(h_hi, wr_lo_ref[...]) + br_ref[...]
    lane = lax.broadcasted_iota(jnp.int32, logits.shape, 1)
    logits = jnp.where(lane < N_EXPERTS, logits, NEG_BIG)
    m1 = jnp.max(logits, axis=-1, keepdims=True)
    i1 = jnp.min(jnp.where(logits == m1, lane, LANES), axis=-1, keepdims=True)
    rest = jnp.where(lane == i1, NEG_BIG, logits)
    m2 = jnp.max(rest, axis=-1, keepdims=True)
    i2 = jnp.min(jnp.where(rest == m2, lane, LANES), axis=-1, keepdims=True)
    e = jnp.exp(m2 - m1)
    w1 = 1.0 / (1.0 + e)
    w2 = e * w1
    info = jnp.where(lane == 0, i1.astype(F32),
                     jnp.where(lane == 1, i2.astype(F32), jnp.where(lane == 2, w1, jnp.where(lane == 3, w2, 0.0))))
    info_ref[0] = info


def _router_call(x, mod, g, wr_hi, wr_lo, br):
    b, s, d = x.shape
    tm = ROW_TILE
    row = pl.BlockSpec((1, tm, d), lambda bi, i: (bi, i, 0))
    full = lambda a: pl.BlockSpec(a.shape, lambda bi, i: (0,) * a.ndim)
    return pl.pallas_call(
        _router_kernel,
        grid=(b, s // tm),
        in_specs=[row, pl.BlockSpec((1, N_MOD, d), lambda bi, i: (bi, 0, 0)), full(g), full(wr_hi), full(wr_lo), full(br)],
        out_specs=[row, pl.BlockSpec((1, tm, LANES), lambda bi, i: (bi, i, 0))],
        out_shape=[jax.ShapeDtypeStruct((b, s, d), F32), jax.ShapeDtypeStruct((b, s, LANES), F32)],
        name="moe_router",
    )(x, mod, g, wr_hi, wr_lo, br)


def _row_copy(src_hbm, dst_vmem, sem, src_row, dst_row):
    return pltpu.make_async_copy(src_hbm.at[pl.ds(src_row, 1), :], dst_vmem.at[pl.ds(dst_row, 1), :], sem)


def _start_rows(idx_ref, src_hbm, dst_vmem, sem):
    def body(r, carry):
        _row_copy(src_hbm, dst_vmem, sem, idx_ref[r], r).start()
        return carry

    lax.fori_loop(0, dst_vmem.shape[0], body, 0)


def _wait_rows(src_hbm, dst_vmem, sem):
    def body(r, carry):
        _row_copy(src_hbm, dst_vmem, sem, 0, r).wait()
        return carry

    lax.fori_loop(0, dst_vmem.shape[0], body, 0)


def _pack_bf16_pairs(v):
    w = v.shape[1] // 2
    hi = lax.bitcast_convert_type(v[:, :w].astype(BF16).astype(F32), jnp.uint32)
    lo = lax.bitcast_convert_type(v[:, w:].astype(BF16).astype(F32), jnp.uint32)
    return hi | (lo >> 16)


def _unpack_bf16_pairs(p):
    hi = lax.bitcast_convert_type(p & jnp.uint32(0xFFFF0000), F32)
    lo = lax.bitcast_convert_type(p << 16, F32)
    return hi, lo


def _expert_kernel(tile_expert_ref, n_valid_ref, idx_ref, idx_next_ref, h_ref, wg_ref, wu_ref, wd_ref, o_ref,
                   rows_scr, xb_scr, acc_scr, sem):
    i = pl.program_id(0)
    j = pl.program_id(1)
    n_valid = n_valid_ref[0]

    @pl.when((i == 0) & (j == 0))
    def _():
        _start_rows(idx_ref, h_ref, rows_scr, sem)

    @pl.when((i < n_valid) & (j == 0))
    def _():
        _wait_rows(h_ref, rows_scr, sem)
        xb_scr[...] = rows_scr[...].astype(BF16)

        @pl.when(i + 1 < n_valid)
        def _():
            _start_rows(idx_next_ref, h_ref, rows_scr, sem)

    @pl.when(i < n_valid)
    def _():
        xb = xb_scr[...]
        gate = _dot(xb, wg_ref[0, 0].astype(BF16))
        up = _dot(xb, wu_ref[0, 0].astype(BF16))
        part = _dot((_silu(gate) * up).astype(BF16), wd_ref[0, 0].astype(BF16))

        @pl.when(j == 0)
        def _():
            acc_scr[...] = part

        @pl.when(j > 0)
        def _():
            acc_scr[...] += part

        @pl.when(j == pl.num_programs(1) - 1)
        def _():
            o_ref[...] = _pack_bf16_pairs(acc_scr[...])

    @pl.when((i >= n_valid) & (j == 0))
    def _():
        o_ref[...] = jnp.zeros(o_ref.shape, o_ref.dtype)


def _expert_call(tile_expert, n_valid, row_token, h_flat, wg, wu, wd, layer_idx):
    n_rows = row_token.shape[0]
    d = h_flat.shape[1]
    tm = EXPERT_ROW_TILE
    tf = EXPERT_FF_TILE
    n_ff = wg.shape[-1]
    n_tiles = n_rows // tm
    last_chunk = n_ff // tf - 1

    def tile(i, nv):
        return jnp.minimum(i, nv[0] - 1)

    def chunk(i, j, nv):
        return jnp.where(i < nv[0], j, last_chunk)

    grid_spec = pltpu.PrefetchScalarGridSpec(
        num_scalar_prefetch=2,
        grid=(n_tiles, n_ff // tf),
        in_specs=[
            pl.BlockSpec((tm,), lambda i, j, te, nv: (tile(i, nv),), memory_space=pltpu.SMEM),
            pl.BlockSpec((tm,), lambda i, j, te, nv: (tile(i + 1, nv),), memory_space=pltpu.SMEM),
            pl.BlockSpec(memory_space=pl.ANY),
            pl.BlockSpec((1, 1, d, tf), lambda i, j, te, nv: (layer_idx, te[tile(i, nv)], 0, chunk(i, j, nv))),
            pl.BlockSpec((1, 1, d, tf), lambda i, j, te, nv: (layer_idx, te[tile(i, nv)], 0, chunk(i, j, nv))),
            pl.BlockSpec((1, 1, tf, d), lambda i, j, te, nv: (layer_idx, te[tile(i, nv)], chunk(i, j, nv), 0)),
        ],
        out_specs=pl.BlockSpec((tm, d // 2), lambda i, j, te, nv: (i, 0)),
        scratch_shapes=[pltpu.VMEM((tm, d), F32), pltpu.VMEM((tm, d), BF16), pltpu.VMEM((tm, d), F32),
                        pltpu.SemaphoreType.DMA(())],
    )
    return pl.pallas_call(
        _expert_kernel,
        grid_spec=grid_spec,
        out_shape=jax.ShapeDtypeStruct((n_rows, d // 2), jnp.uint32),
        name="moe_experts",
    )(tile_expert, n_valid, row_token, row_token, h_flat, wg, wu, wd)


def _combine_kernel(p1_ref, p2_ref, x_ref, info_ref, gate_ref, gfin_ref, y_ref, o_ref, buf1, buf2, sem1, sem2, *, final_norm):
    n = x_ref.shape[0]

    def start(r, _):
        _row_copy(y_ref, buf1, sem1, p1_ref[r], r).start()
        _row_copy(y_ref, buf2, sem2, p2_ref[r], r).start()
        return 0

    def wait(r, _):
        _row_copy(y_ref, buf1, sem1, 0, r).wait()
        _row_copy(y_ref, buf2, sem2, 0, r).wait()
        return 0

    lax.fori_loop(0, n, start, 0)
    lax.fori_loop(0, n, wait, 0)
    info = info_ref[...]
    a_hi, a_lo = _unpack_bf16_pairs(buf1[...])
    b_hi, b_lo = _unpack_bf16_pairs(buf2[...])
    w1 = info[:, 2:3]
    w2 = info[:, 3:4]
    y = jnp.concatenate([w1 * a_hi + w2 * b_hi, w1 * a_lo + w2 * b_lo], axis=1)
    out = x_ref[...] + gate_ref[0] * y
    if final_norm:
        out = _rms_rows(out) * gfin_ref[...]
    o_ref[...] = out


def _combine_call(p1, p2, x_flat, info_flat, gate_rows, g_final, y_sorted, tokens_per_batch, final_norm):
    t, d = x_flat.shape
    tc = GATHER_TILE
    per_b = tokens_per_batch // tc
    idx = pl.BlockSpec((tc,), lambda i: (i,), memory_space=pltpu.SMEM)
    return pl.pallas_call(
        functools.partial(_combine_kernel, final_norm=final_norm),
        grid=(t // tc,),
        in_specs=[idx, idx, pl.BlockSpec((tc, d), lambda i: (i, 0)), pl.BlockSpec((tc, LANES), lambda i: (i, 0)),
                  pl.BlockSpec((1, 1, d), lambda i: (i // per_b, 0, 0)), pl.BlockSpec((1, d), lambda i: (0, 0)),
                  pl.BlockSpec(memory_space=pl.ANY)],
        out_specs=pl.BlockSpec((tc, d), lambda i: (i, 0)),
        out_shape=jax.ShapeDtypeStruct((t, d), F32),
        scratch_shapes=[pltpu.VMEM((tc, d // 2), jnp.uint32), pltpu.VMEM((tc, d // 2), jnp.uint32),
                        pltpu.SemaphoreType.DMA(()), pltpu.SemaphoreType.DMA(())],
        name="moe_combine",
    )(p1, p2, x_flat, info_flat, gate_rows, g_final, y_sorted)


def _routing_tables(expert_ids, n_tokens):
    tm = EXPERT_ROW_TILE
    n_assign = n_tokens * TOP_K
    n_rows = n_assign + N_EXPERTS * tm
    e_flat = expert_ids.reshape(n_assign)
    onehot = (e_flat[:, None] == jnp.arange(N_EXPERTS, dtype=jnp.int32)[None, :]).astype(jnp.int32)
    rank = jnp.sum((jnp.cumsum(onehot, axis=0) - onehot) * onehot, axis=1)
    counts = jnp.sum(onehot, axis=0)
    padded = ((counts + tm - 1) // tm) * tm
    ends = jnp.cumsum(padded)
    starts = ends - padded
    pos = (starts[e_flat] + rank).astype(jnp.int32)
    row_token = jnp.zeros((n_rows,), jnp.int32).at[pos].set(jnp.arange(n_assign, dtype=jnp.int32) // TOP_K)
    n_valid = (ends[-1] // tm).astype(jnp.int32).reshape(1)
    tile_start = jnp.arange(n_rows // tm, dtype=jnp.int32) * tm
    tile_expert = jnp.minimum(jnp.sum((ends[None, :] <= tile_start[:, None]).astype(jnp.int32), axis=1), N_EXPERTS - 1)
    pos2 = pos.reshape(n_tokens, TOP_K)
    return row_token, tile_expert, n_valid, pos2[:, 0], pos2[:, 1]


def _prep_w_in(w_in):
    d = w_in.shape[0]
    cq, ckv, kr, qsb, ksb, vsb = jnp.split(
        w_in, (Q_LORA, Q_LORA + KV_LORA, Q_LORA + KV_LORA + MLA_ROPE,
               Q_LORA + KV_LORA + MLA_ROPE + SB_WIDTH, Q_LORA + KV_LORA + MLA_ROPE + 2 * SB_WIDTH), axis=1)
    z = lambda n: jnp.zeros((d, n), w_in.dtype)
    kr_blk = jnp.concatenate([z(MLA_NOPE), kr, z(LANES - MLA_QK_DIM)], axis=1)
    kr_swap = jnp.concatenate([z(MLA_NOPE), kr[:, HALF_ROPE:], kr[:, :HALF_ROPE], z(LANES - MLA_QK_DIM)], axis=1)
    return jnp.concatenate([cq, ckv, kr_blk, kr_swap, qsb * SB_HEAD_DIM ** -0.5, ksb, vsb], axis=1).astype(BF16)


def _prep_w_uq(w_uq):
    r = w_uq.shape[0]
    w = w_uq.reshape(r, MLA_HEADS, MLA_QK_DIM)
    nope, x1, x2 = w[..., :MLA_NOPE], w[..., MLA_NOPE:MLA_NOPE + HALF_ROPE], w[..., MLA_NOPE + HALF_ROPE:]
    pad = jnp.zeros((r, MLA_HEADS, LANES - MLA_QK_DIM), w.dtype)
    plain = jnp.concatenate([nope, x1, x2, pad], axis=-1).reshape(r, MLA_PAD_WIDTH)
    swapped = jnp.concatenate([jnp.zeros_like(nope), x2, x1, pad], axis=-1).reshape(r, MLA_PAD_WIDTH)
    return jnp.concatenate([plain, swapped], axis=1).astype(BF16)


def _prep_w_ukv(w_ukv):
    r = w_ukv.shape[0]
    w = w_ukv.reshape(r, MLA_HEADS, MLA_NOPE + MLA_V)
    k_nope, v = w[..., :MLA_NOPE], w[..., MLA_NOPE:]
    k_pad = jnp.concatenate([k_nope, jnp.zeros((r, MLA_HEADS, LANES - MLA_NOPE), w.dtype)], axis=-1)
    return jnp.concatenate([k_pad.reshape(r, MLA_PAD_WIDTH), v.reshape(r, MLA_WIDTH)], axis=1).astype(BF16)


def kernel(x, c, positions, w_ada, b_ada, g_mix, g_ffn, w_in, g_q_lat, w_uq, g_kv_lat, w_ukv, g_grp_mla, g_grp_sb, w_out, w_ff_gate, w_ff_up, w_ff_down, w_router, b_router, w_ex_gate, w_ex_up, w_ex_down, g_final):
    b, s, d = x.shape
    depth = w_ada.shape[0]
    n_tokens = b * s
    assert depth % 2 == 0, "the final norm is fused into the routed mixer of the last layer"

    c_pad = jnp.zeros((8, d), F32).at[:b].set(c)
    mod_all = _ada_call(c_pad, w_ada, b_ada)[:, :b].reshape(depth, b, N_MOD, d)

    lane = jnp.arange(LANES)
    freq = ROPE_BASE ** (-((lane - MLA_NOPE) % HALF_ROPE).astype(F32) / HALF_ROPE)
    pos_lanes = jnp.broadcast_to(positions.astype(F32)[..., None], (b, s, LANES))
    cos_t, sin_t = _rope_tables(pos_lanes, freq.reshape(1, LANES))

    row2 = lambda v: v.reshape(1, -1)
    for layer in range(depth):
        mod = mod_all[layer]
        qm, km, vtm, qs, ks, vts = _proj_call(
            x, mod, row2(g_mix[layer]), cos_t, sin_t, _prep_w_in(w_in[layer]), row2(g_q_lat[layer]),
            _prep_w_uq(w_uq[layer]), row2(g_kv_lat[layer]), _prep_w_ukv(w_ukv[layer]))
        o_mla = _mla_call(qm, km, vtm)
        o_sb = _sb_call(qs, ks, vts)
        x = _merge_call(x, o_mla, o_sb, mod, row2(g_grp_mla[layer]), row2(g_grp_sb[layer]), w_out[layer].astype(BF16))

        i = layer // 2
        if layer % 2 == 0:
            x = _ffn_call(x, mod, row2(g_ffn[layer]), w_ff_gate[i].astype(BF16), w_ff_up[i].astype(BF16),
                          w_ff_down[i].astype(BF16))
        else:
            wr = jnp.zeros((d, LANES), F32).at[:, :N_EXPERTS].set(w_router[i])
            wr_hi = wr.astype(BF16)
            wr_lo = (wr - wr_hi.astype(F32)).astype(BF16)
            br = jnp.zeros((1, LANES), F32).at[0, :N_EXPERTS].set(b_router[i])
            h, info = _router_call(x, mod, row2(g_ffn[layer]), wr_hi, wr_lo, br)
            info_flat = info.reshape(n_tokens, LANES)
            expert_ids = info_flat[:, :TOP_K].astype(jnp.int32)
            row_token, tile_expert, n_valid, p1, p2 = _routing_tables(expert_ids, n_tokens)
            ys = _expert_call(tile_expert, n_valid, row_token, h.reshape(n_tokens, d), w_ex_gate, w_ex_up, w_ex_down, i)
            final = layer == depth - 1
            x = _combine_call(p1, p2, x.reshape(n_tokens, d), info_flat, mod[:, 5:6, :], row2(g_final), ys, s,
                              final_norm=final).reshape(b, s, d)
    return x
```

```python
import functools

import jax
import jax.numpy as jnp
from jax import lax
from jax.experimental import pallas as pl
from jax.experimental.pallas import tpu as pltpu

F32 = jnp.float32
BF16 = jnp.bfloat16

EPS = 1e-6
N_MOD = 6
CHUNK = 64
MLA_HEADS = 8
MLA_NOPE = 64
MLA_ROPE = 32
MLA_V = 64
Q_LORA = 256
KV_LORA = 128
ROPE_BASE = 10000.0
SB_HEADS = 8
SB_HEAD_DIM = 64
N_EXPERTS = 8
TOP_K = 2

LANES = 128
HALF_ROPE = MLA_ROPE // 2
MLA_QK_DIM = MLA_NOPE + MLA_ROPE
LOG2E = 1.4426950408889634
NEG_BIG = -1e30

ROW_TILE = 512
ATT_TILE = 256
EXPERT_ROW_TILE = 1024
EXPERT_FF_TILE = 512
GATHER_TILE = 1024
FFN_CHUNK = 512
DMA_UNROLL = 8


def _split_bf16(v):
    hi = v.astype(BF16)
    lo = (v - hi.astype(F32)).astype(BF16)
    return hi, lo


def _dot(a, b):
    return jnp.dot(a, b, preferred_element_type=F32)


def _rms_rows(v):
    return v * lax.rsqrt(jnp.mean(v * v, axis=-1, keepdims=True) + EPS)


def _modulated_norm(x, gain, scale, shift):
    return _rms_rows(x) * (gain * (1.0 + scale)) + shift


def _ada_kernel(c_ref, w_ref, b_ref, o_ref):
    c = c_ref[...]
    act = c * (1.0 / (1.0 + jnp.exp(-c)))
    a_hi, a_lo = _split_bf16(act)
    w_hi, w_lo = _split_bf16(w_ref[0])
    o_ref[0] = _dot(a_hi, w_hi) + _dot(a_lo, w_hi) + _dot(a_hi, w_lo) + b_ref[0]


def _ada_call(c_pad, w_ada, b_ada):
    depth, d, n = w_ada.shape
    tn = n // 4
    return pl.pallas_call(
        _ada_kernel,
        grid=(depth, n // tn),
        in_specs=[
            pl.BlockSpec((8, d), lambda l, j: (0, 0)),
            pl.BlockSpec((1, d, tn), lambda l, j: (l, 0, j)),
            pl.BlockSpec((1, 1, tn), lambda l, j: (l, 0, j)),
        ],
        out_specs=pl.BlockSpec((1, 8, tn), lambda l, j: (l, 0, j)),
        out_shape=jax.ShapeDtypeStruct((depth, 8, n), F32),
        name="ada_mod",
    )(c_pad, w_ada, b_ada.reshape(depth, 1, n))


def _rope_table_kernel(pos_ref, freq_ref, cos_ref, sin_ref):
    ang = pos_ref[0] * freq_ref[...]
    lane = lax.broadcasted_iota(jnp.int32, ang.shape, 1)
    in_rope = (lane >= MLA_NOPE) & (lane < MLA_QK_DIM)
    first_half = lane < MLA_NOPE + HALF_ROPE
    cos_ref[0] = jnp.where(in_rope, jnp.cos(ang), jnp.where(lane < MLA_NOPE, 1.0, 0.0))
    s = jnp.sin(ang)
    sin_ref[0] = jnp.where(in_rope, jnp.where(first_half, -s, s), 0.0)


def _rope_tables(pos_lanes, freq_lanes):
    b, s, _ = pos_lanes.shape
    tm = ROW_TILE
    spec = pl.BlockSpec((1, tm, LANES), lambda bi, i: (bi, i, 0))
    return pl.pallas_call(
        _rope_table_kernel,
        grid=(b, s // tm),
        in_specs=[spec, pl.BlockSpec((1, LANES), lambda bi, i: (0, 0))],
        out_specs=[spec, spec],
        out_shape=[jax.ShapeDtypeStruct((b, s, LANES), F32)] * 2,
        name="rope_tables",
    )(pos_lanes, freq_lanes)


IN_OFF_CQ = 0
IN_OFF_CKV = Q_LORA
IN_OFF_KR = IN_OFF_CKV + KV_LORA
IN_OFF_KRS = IN_OFF_KR + LANES
IN_OFF_QSB = IN_OFF_KRS + LANES
SB_WIDTH = SB_HEADS * SB_HEAD_DIM
MLA_WIDTH = MLA_HEADS * MLA_V
IN_OFF_KSB = IN_OFF_QSB + SB_WIDTH
IN_OFF_VSB = IN_OFF_KSB + SB_WIDTH
IN_COLS_PADDED = IN_OFF_VSB + SB_WIDTH
MLA_PAD_WIDTH = MLA_HEADS * LANES
MLA_V_EXT = MLA_V + 16


def _proj_kernel(x_ref, mod_ref, gmix_ref, cos_ref, sin_ref, win_ref, gq_ref, wuq_ref, gkv_ref, wukv_ref,
                 qm_ref, km_ref, vtm_ref, qs_ref, ks_ref, vts_ref):
    x = x_ref[0]
    mod = mod_ref[0]
    h = _modulated_norm(x, gmix_ref[...], mod[1:2], mod[0:1])
    proj = _dot(h.astype(BF16), win_ref[...])
    cos = cos_ref[0]
    sin = sin_ref[0]
    n_kv = vtm_ref.shape[1]
    tk = vtm_ref.shape[3]

    cqn = _rms_rows(proj[:, IN_OFF_CQ:IN_OFF_CQ + Q_LORA]) * gq_ref[...]
    q2 = _dot(cqn.astype(BF16), wuq_ref[...])
    q_scale = MLA_QK_DIM ** -0.5 * LOG2E
    for hd in range(MLA_HEADS):
        lo = hd * LANES
        q = q2[:, lo:lo + LANES] * cos + q2[:, MLA_PAD_WIDTH + lo:MLA_PAD_WIDTH + lo + LANES] * sin
        qm_ref[0, :, lo:lo + LANES] = (q * q_scale).astype(qm_ref.dtype)

    ckvn = _rms_rows(proj[:, IN_OFF_CKV:IN_OFF_CKV + KV_LORA]) * gkv_ref[...]
    kv2 = _dot(ckvn.astype(BF16), wukv_ref[...])
    k_rope = proj[:, IN_OFF_KR:IN_OFF_KR + LANES] * cos + proj[:, IN_OFF_KRS:IN_OFF_KRS + LANES] * sin
    for hd in range(MLA_HEADS):
        lo = hd * LANES
        km_ref[0, :, lo:lo + LANES] = (kv2[:, lo:lo + LANES] + k_rope).astype(km_ref.dtype)
    v_t = kv2[:, MLA_PAD_WIDTH:].T
    pad_rows = MLA_V_EXT - MLA_V
    ones_row = jnp.where(lax.broadcasted_iota(jnp.int32, (pad_rows, tk), 0) == 0, 1.0, 0.0).astype(vtm_ref.dtype)
    for c in range(n_kv):
        for hd in range(MLA_HEADS):
            lo = hd * MLA_V_EXT
            vtm_ref[0, c, lo:lo + MLA_V, :] = v_t[hd * MLA_V:(hd + 1) * MLA_V, c * tk:(c + 1) * tk].astype(vtm_ref.dtype)
            vtm_ref[0, c, lo + MLA_V:lo + MLA_V_EXT, :] = ones_row

    qs_ref[0] = proj[:, IN_OFF_QSB:IN_OFF_QSB + SB_WIDTH].astype(qs_ref.dtype)
    ks_ref[0] = proj[:, IN_OFF_KSB:IN_OFF_KSB + SB_WIDTH].astype(ks_ref.dtype)
    vs_t = proj[:, IN_OFF_VSB:IN_OFF_VSB + SB_WIDTH].T
    for c in range(n_kv):
        vts_ref[0, c] = vs_t[:, c * tk:(c + 1) * tk].astype(vts_ref.dtype)


def _proj_call(x, mod, gmix, cos_t, sin_t, win, gq, wuq, gkv, wukv):
    b, s, d = x.shape
    tm = ROW_TILE
    tk = ATT_TILE
    n_kv = tm // tk
    row = lambda w: pl.BlockSpec((1, tm, w), lambda bi, i: (bi, i, 0))
    full = lambda a: pl.BlockSpec(a.shape, lambda bi, i: (0,) * a.ndim)
    vt_spec = lambda w: pl.BlockSpec((1, n_kv, w, tk), lambda bi, i: (bi, i, 0, 0))
    return pl.pallas_call(
        _proj_kernel,
        grid=(b, s // tm),
        in_specs=[row(d), pl.BlockSpec((1, N_MOD, d), lambda bi, i: (bi, 0, 0)), full(gmix), row(LANES), row(LANES),
                  full(win), full(gq), full(wuq), full(gkv), full(wukv)],
        out_specs=[row(MLA_PAD_WIDTH), row(MLA_PAD_WIDTH), vt_spec(MLA_HEADS * MLA_V_EXT),
                   row(SB_WIDTH), row(SB_WIDTH), vt_spec(SB_WIDTH)],
        out_shape=[
            jax.ShapeDtypeStruct((b, s, MLA_PAD_WIDTH), BF16),
            jax.ShapeDtypeStruct((b, s, MLA_PAD_WIDTH), BF16),
            jax.ShapeDtypeStruct((b, s // tk, MLA_HEADS * MLA_V_EXT, tk), BF16),
            jax.ShapeDtypeStruct((b, s, SB_WIDTH), BF16),
            jax.ShapeDtypeStruct((b, s, SB_WIDTH), BF16),
            jax.ShapeDtypeStruct((b, s // tk, SB_WIDTH, tk), BF16),
        ],
        name="in_proj",
    )(x, mod, gmix, cos_t, sin_t, win, gq, wuq, gkv, wukv)


def _resident(shape, index_map):
    return pl.BlockSpec(shape, index_map, pipeline_mode=pl.Buffered(1))


def _mla_kernel(q_ref, k_ref, vt_ref, o_ref, qt_scr, sc_a, sc_b, m_scr, acc_scr):
    tq = q_ref.shape[1]
    tk = vt_ref.shape[3]
    i = pl.program_id(1)
    for hd in range(MLA_HEADS):
        qt_scr[hd] = q_ref[0, :, hd * LANES:(hd + 1) * LANES].astype(F32).T.astype(BF16)
    m_scr[...] = jnp.full(m_scr.shape, NEG_BIG, F32)
    acc_scr[...] = jnp.zeros(acc_scr.shape, F32)

    def scores_into(j, sc):
        start = pl.multiple_of(j * tk, tk)
        for hd in range(MLA_HEADS):
            sc[hd] = _dot(k_ref[0, pl.ds(start, tk), hd * LANES:(hd + 1) * LANES], qt_scr[hd])

    def softmax_pv(j, sc, masked):
        probs = []
        alphas = []
        for hd in range(MLA_HEADS):
            s = sc[hd]
            if masked:
                key_chunk = lax.broadcasted_iota(jnp.int32, s.shape, 0) // CHUNK
                qry_chunk = lax.broadcasted_iota(jnp.int32, s.shape, 1) // CHUNK
                s = jnp.where(key_chunk <= qry_chunk, s, NEG_BIG)
            m = m_scr[hd]
            m_new = jnp.maximum(m, jnp.max(s, axis=0, keepdims=True))
            m_scr[hd] = m_new
            alphas.append(jnp.exp2(m - m_new))
            probs.append(jnp.exp2(s - m_new).astype(BF16))
        for hd in range(MLA_HEADS):
            pv = _dot(vt_ref[0, j, hd * MLA_V_EXT:(hd + 1) * MLA_V_EXT, :], probs[hd])
            acc_scr[hd] = alphas[hd] * acc_scr[hd] + pv

    def two_blocks(j):
        scores_into(j + 1, sc_b)
        softmax_pv(j, sc_a, masked=False)
        scores_into(j + 2, sc_a)
        softmax_pv(j + 1, sc_b, masked=False)

    def four_blocks(t, carry):
        two_blocks(4 * t)
        two_blocks(4 * t + 2)
        return carry

    scores_into(0, sc_a)
    lax.fori_loop(0, i // 4, four_blocks, 0)
    rem = i % 4

    @pl.when(rem >= 2)
    def _():
        two_blocks(i - rem)

    @pl.when(rem % 2 == 1)
    def _():
        scores_into(i, sc_b)
        softmax_pv(i - 1, sc_a, masked=False)
        softmax_pv(i, sc_b, masked=True)

    @pl.when(rem % 2 == 0)
    def _():
        softmax_pv(i, sc_a, masked=True)

    for hd in range(MLA_HEADS):
        acc = acc_scr[hd]
        o_ref[0, hd * MLA_V:(hd + 1) * MLA_V, :] = acc[:MLA_V] * (1.0 / acc[MLA_V:MLA_V + 1])


def _mla_call(q, k, vt):
    b, s, _ = q.shape
    n_kv, tk = vt.shape[1], vt.shape[3]
    tq = tk
    scores = pltpu.VMEM((MLA_HEADS, tk, tq), F32)
    return pl.pallas_call(
        _mla_kernel,
        grid=(b, s // tq),
        in_specs=[
            pl.BlockSpec((1, tq, MLA_PAD_WIDTH), lambda bi, i: (bi, i, 0)),
            _resident((1, s, MLA_PAD_WIDTH), lambda bi, i: (bi, 0, 0)),
            _resident((1, n_kv, MLA_HEADS * MLA_V_EXT, tk), lambda bi, i: (bi, 0, 0, 0)),
        ],
        out_specs=pl.BlockSpec((1, MLA_WIDTH, tq), lambda bi, i: (bi, 0, i)),
        out_shape=jax.ShapeDtypeStruct((b, MLA_WIDTH, s), F32),
        scratch_shapes=[pltpu.VMEM((MLA_HEADS, LANES, tq), BF16), scores, scores,
                        pltpu.VMEM((MLA_HEADS, 1, tq), F32), pltpu.VMEM((MLA_HEADS, MLA_V_EXT, tq), F32)],
        name="mla_attention",
    )(q, k, vt)


F32_EXP2_ZERO_BELOW = -152.0


def _sb_kernel(q_ref, k_ref, vt_ref, o_ref, qt_scr, later_scr, run_scr, acc_scr):
    tq = q_ref.shape[1]
    tk = vt_ref.shape[3]
    i = pl.program_id(1)
    for pair in range(SB_HEADS // 2):
        q_t = q_ref[0, :, pair * LANES:(pair + 1) * LANES].astype(F32).T * LOG2E
        head_row = lax.broadcasted_iota(jnp.int32, q_t.shape, 0) // SB_HEAD_DIM
        for sub in range(2):
            qt_scr[2 * pair + sub] = jnp.where(head_row == sub, q_t, 0.0).astype(BF16)
    r = lax.broadcasted_iota(jnp.int32, later_scr.shape, 0)
    c = lax.broadcasted_iota(jnp.int32, later_scr.shape, 1) % tk
    later_scr[...] = jnp.where(((r < tk) & (c > r)) | (r == tk), 1.0, 0.0).astype(BF16)
    run_scr[...] = jnp.zeros(run_scr.shape, F32)
    acc_scr[...] = jnp.zeros(acc_scr.shape, F32)

    def step(j, masked):
        start = pl.multiple_of(j * tk, tk)
        zs = [_dot(k_ref[0, pl.ds(start, tk), (hd // 2) * LANES:(hd // 2 + 1) * LANES], qt_scr[hd])
              for hd in range(SB_HEADS)]
        if masked:
            strict = lax.broadcasted_iota(jnp.int32, (tk, tq), 0) < lax.broadcasted_iota(jnp.int32, (tk, tq), 1)
        log_betas, laters = [], []
        for hd in range(SB_HEADS):
            z = zs[hd]
            soft = jnp.log2(1.0 + jnp.exp2(-jnp.abs(z)))
            drop = jnp.maximum(z, 0.0) + soft
            log_betas.append(jnp.minimum(z, 0.0) - soft)
            if masked:
                drop = jnp.where(strict, drop, 0.0)
            drop_hi, drop_lo = _split_bf16(drop)
            laters.append(_dot(later_scr[...], jnp.concatenate([drop_hi, drop_lo], axis=0)))
        weights = []
        for hd in range(SB_HEADS):
            run = run_scr[hd]
            a = jnp.exp2(log_betas[hd] - laters[hd][:tk] - run)
            if masked:
                a = jnp.where(strict, a, 0.0)
            weights.append(a.astype(BF16))
            run_scr[hd] = run + laters[hd][tk:tk + 1]
        for hd in range(SB_HEADS):
            acc_scr[hd] += _dot(vt_ref[0, j, hd * SB_HEAD_DIM:(hd + 1) * SB_HEAD_DIM, :], weights[hd])

    def any_weight_left():
        return (jnp.min(run_scr[...]) < -F32_EXP2_ZERO_BELOW).astype(jnp.int32)

    step(i, masked=True)

    def cond(carry):
        j, alive = carry
        return (j >= 0) & (alive > 0)

    def body(carry):
        j, _ = carry
        step(j, masked=False)
        return j - 1, any_weight_left()

    lax.while_loop(cond, body, (i - 1, any_weight_left()))
    for hd in range(SB_HEADS):
        o_ref[0, hd * SB_HEAD_DIM:(hd + 1) * SB_HEAD_DIM, :] = acc_scr[hd]


def _sb_call(q, k, vt):
    b, s, _ = q.shape
    n_kv, tk = vt.shape[1], vt.shape[3]
    tq = tk
    return pl.pallas_call(
        _sb_kernel,
        grid=(b, s // tq),
        in_specs=[
            pl.BlockSpec((1, tq, SB_WIDTH), lambda bi, i: (bi, i, 0)),
            _resident((1, s, SB_WIDTH), lambda bi, i: (bi, 0, 0)),
            _resident((1, n_kv, SB_WIDTH, tk), lambda bi, i: (bi, 0, 0, 0)),
        ],
        out_specs=pl.BlockSpec((1, SB_WIDTH, tq), lambda bi, i: (bi, 0, i)),
        out_shape=jax.ShapeDtypeStruct((b, SB_WIDTH, s), F32),
        scratch_shapes=[pltpu.VMEM((SB_HEADS, LANES, tq), BF16), pltpu.VMEM((tk + 16, 2 * tk), BF16),
                        pltpu.VMEM((SB_HEADS, 1, tq), F32), pltpu.VMEM((SB_HEADS, SB_HEAD_DIM, tq), F32)],
        name="sb_attention",
    )(q, k, vt)


def _merge_kernel(x_ref, om_ref, os_ref, mod_ref, gm_ref, gs_ref, wout_ref, o_ref):
    def group(o_t, gain):
        inv = lax.rsqrt(jnp.mean(o_t * o_t, axis=0, keepdims=True) + EPS)
        return ((o_t * inv).T * gain).astype(BF16)

    a = group(om_ref[0], gm_ref[...])
    bb = group(os_ref[0], gs_ref[...])
    mixed = _dot(a, wout_ref[:MLA_WIDTH, :]) + _dot(bb, wout_ref[MLA_WIDTH:, :])
    o_ref[0] = x_ref[0] + mod_ref[0][2:3] * mixed


def _merge_call(x, o_mla, o_sb, mod, g_mla, g_sb, wout):
    b, s, d = x.shape
    tm = ROW_TILE
    row = pl.BlockSpec((1, tm, d), lambda bi, i: (bi, i, 0))
    col = lambda w: pl.BlockSpec((1, w, tm), lambda bi, i: (bi, 0, i))
    full = lambda a: pl.BlockSpec(a.shape, lambda bi, i: (0,) * a.ndim)
    return pl.pallas_call(
        _merge_kernel,
        grid=(b, s // tm),
        in_specs=[row, col(MLA_WIDTH), col(SB_WIDTH), pl.BlockSpec((1, N_MOD, d), lambda bi, i: (bi, 0, 0)),
                  full(g_mla), full(g_sb), full(wout)],
        out_specs=row,
        out_shape=jax.ShapeDtypeStruct((b, s, d), F32),
        name="merge_out_proj",
    )(x, o_mla, o_sb, mod, g_mla, g_sb, wout)


def _silu(v):
    return v * (1.0 / (1.0 + jnp.exp(-v)))


def _ffn_kernel(x_ref, mod_ref, g_ref, wg_ref, wu_ref, wd_ref, o_ref):
    x = x_ref[0]
    mod = mod_ref[0]
    hb = _modulated_norm(x, g_ref[...], mod[4:5], mod[3:4]).astype(BF16)
    acc = jnp.zeros(x.shape, F32)
    for c in range(wg_ref.shape[1] // FFN_CHUNK):
        lo = c * FFN_CHUNK
        gate = _dot(hb, wg_ref[:, lo:lo + FFN_CHUNK])
        up = _dot(hb, wu_ref[:, lo:lo + FFN_CHUNK])
        acc = acc + _dot((_silu(gate) * up).astype(BF16), wd_ref[lo:lo + FFN_CHUNK, :])
    o_ref[0] = x + mod[5:6] * acc


def _ffn_call(x, mod, g, wg, wu, wd):
    b, s, d = x.shape
    tm = ROW_TILE
    row = pl.BlockSpec((1, tm, d), lambda bi, i: (bi, i, 0))
    full = lambda a: pl.BlockSpec(a.shape, lambda bi, i: (0,) * a.ndim)
    return pl.pallas_call(
        _ffn_kernel,
        grid=(b, s // tm),
        in_specs=[row, pl.BlockSpec((1, N_MOD, d), lambda bi, i: (bi, 0, 0)), full(g), full(wg), full(wu), full(wd)],
        out_specs=row,
        out_shape=jax.ShapeDtypeStruct((b, s, d), F32),
        name="dense_ffn",
    )(x, mod, g, wg, wu, wd)


def _router_kernel(x_ref, mod_ref, g_ref, wr_hi_ref, wr_lo_ref, br_ref, h_ref, info_ref):
    x = x_ref[0]
    mod = mod_ref[0]
    h = _modulated_norm(x, g_ref[...], mod[4:5], mod[3:4])
    h_ref[0] = h
    h_hi, h_lo = _split_bf16(h)
    w_hi = wr_hi_ref[...]
    logits = _dot(h_hi, w_hi) + _dot(h_lo, w_hi) + _dot(h_hi, wr_lo_ref[...]) + br_ref[...]
    lane = lax.broadcasted_iota(jnp.int32, logits.shape, 1)
    logits = jnp.where(lane < N_EXPERTS, logits, NEG_BIG)
    m1 = jnp.max(logits, axis=-1, keepdims=True)
    i1 = jnp.min(jnp.where(logits == m1, lane, LANES), axis=-1, keepdims=True)
    rest = jnp.where(lane == i1, NEG_BIG, logits)
    m2 = jnp.max(rest, axis=-1, keepdims=True)
    i2 = jnp.min(jnp.where(rest == m2, lane, LANES), axis=-1, keepdims=True)
    e = jnp.exp(m2 - m1)
    w1 = 1.0 / (1.0 + e)
    w2 = e * w1
    info = jnp.where(lane == 0, i1.astype(F32),
                     jnp.where(lane == 1, i2.astype(F32), jnp.where(lane == 2, w1, jnp.where(lane == 3, w2, 0.0))))
    info_ref[0] = info


def _router_call(x, mod, g, wr_hi, wr_lo, br):
    b, s, d = x.shape
    tm = ROW_TILE
    row = pl.BlockSpec((1, tm, d), lambda bi, i: (bi, i, 0))
    full = lambda a: pl.BlockSpec(a.shape, lambda bi, i: (0,) * a.ndim)
    return pl.pallas_call(
        _router_kernel,
        grid=(b, s // tm),
        in_specs=[row, pl.BlockSpec((1, N_MOD, d), lambda bi, i: (bi, 0, 0)), full(g), full(wr_hi), full(wr_lo), full(br)],
        out_specs=[row, pl.BlockSpec((1, tm, LANES), lambda bi, i: (bi, i, 0))],
        out_shape=[jax.ShapeDtypeStruct((b, s, d), F32), jax.ShapeDtypeStruct((b, s, LANES), F32)],
        name="moe_router",
    )(x, mod, g, wr_hi, wr_lo, br)


def _row_copy(src_hbm, dst_vmem, sem, src_row, dst_row):
    return pltpu.make_async_copy(src_hbm.at[pl.ds(src_row, 1), :], dst_vmem.at[pl.ds(dst_row, 1), :], sem)


def _start_rows(idx_ref, src_hbm, dst_vmem, sem):
    def body(r, carry):
        _row_copy(src_hbm, dst_vmem, sem, idx_ref[r], r).start()
        return carry

    lax.fori_loop(0, dst_vmem.shape[0], body, 0, unroll=DMA_UNROLL)


def _wait_rows(src_hbm, dst_vmem, sem):
    def body(r, carry):
        _row_copy(src_hbm, dst_vmem, sem, 0, r).wait()
        return carry

    lax.fori_loop(0, dst_vmem.shape[0], body, 0, unroll=DMA_UNROLL)


def _pack_bf16_pairs(v):
    w = v.shape[1] // 2
    hi = lax.bitcast_convert_type(v[:, :w].astype(BF16).astype(F32), jnp.uint32)
    lo = lax.bitcast_convert_type(v[:, w:].astype(BF16).astype(F32), jnp.uint32)
    return hi | (lo >> 16)


def _unpack_bf16_pairs(p):
    hi = lax.bitcast_convert_type(p & jnp.uint32(0xFFFF0000), F32)
    lo = lax.bitcast_convert_type(p << 16, F32)
    return hi, lo


def _expert_kernel(tile_expert_ref, n_valid_ref, idx_ref, idx_next_ref, h_ref, wg_ref, wu_ref, wd_ref, o_ref,
                   rows_scr, xb_scr, acc_scr, sem):
    i = pl.program_id(0)
    j = pl.program_id(1)
    n_valid = n_valid_ref[0]

    @pl.when((i == 0) & (j == 0))
    def _():
        _start_rows(idx_ref, h_ref, rows_scr, sem)

    @pl.when((i < n_valid) & (j == 0))
    def _():
        _wait_rows(h_ref, rows_scr, sem)
        xb_scr[...] = rows_scr[...].astype(BF16)

        @pl.when(i + 1 < n_valid)
        def _():
            _start_rows(idx_next_ref, h_ref, rows_scr, sem)

    @pl.when(i < n_valid)
    def _():
        xb = xb_scr[...]
        gate = _dot(xb, wg_ref[0, 0].astype(BF16))
        up = _dot(xb, wu_ref[0, 0].astype(BF16))
        part = _dot((_silu(gate) * up).astype(BF16), wd_ref[0, 0].astype(BF16))

        @pl.when(j == 0)
        def _():
            acc_scr[...] = part

        @pl.when(j > 0)
        def _():
            acc_scr[...] += part

        @pl.when(j == pl.num_programs(1) - 1)
        def _():
            o_ref[...] = _pack_bf16_pairs(acc_scr[...])

    @pl.when((i >= n_valid) & (j == 0))
    def _():
        o_ref[...] = jnp.zeros(o_ref.shape, o_ref.dtype)


def _expert_call(tile_expert, n_valid, row_token, h_flat, wg, wu, wd, layer_idx):
    n_rows = row_token.shape[0]
    d = h_flat.shape[1]
    tm = EXPERT_ROW_TILE
    tf = EXPERT_FF_TILE
    n_ff = wg.shape[-1]
    n_tiles = n_rows // tm
    last_chunk = n_ff // tf - 1

    def tile(i, nv):
        return jnp.minimum(i, nv[0] - 1)

    def chunk(i, j, nv):
        return jnp.where(i < nv[0], j, last_chunk)

    grid_spec = pltpu.PrefetchScalarGridSpec(
        num_scalar_prefetch=2,
        grid=(n_tiles, n_ff // tf),
        in_specs=[
            pl.BlockSpec((tm,), lambda i, j, te, nv: (tile(i, nv),), memory_space=pltpu.SMEM),
            pl.BlockSpec((tm,), lambda i, j, te, nv: (tile(i + 1, nv),), memory_space=pltpu.SMEM),
            pl.BlockSpec(memory_space=pl.ANY),
            pl.BlockSpec((1, 1, d, tf), lambda i, j, te, nv: (layer_idx, te[tile(i, nv)], 0, chunk(i, j, nv))),
            pl.BlockSpec((1, 1, d, tf), lambda i, j, te, nv: (layer_idx, te[tile(i, nv)], 0, chunk(i, j, nv))),
            pl.BlockSpec((1, 1, tf, d), lambda i, j, te, nv: (layer_idx, te[tile(i, nv)], chunk(i, j, nv), 0)),
        ],
        out_specs=pl.BlockSpec((tm, d // 2), lambda i, j, te, nv: (i, 0)),
        scratch_shapes=[pltpu.VMEM((tm, d), F32), pltpu.VMEM((tm, d), BF16), pltpu.VMEM((tm, d), F32),
                        pltpu.SemaphoreType.DMA(())],
    )
    return pl.pallas_call(
        _expert_kernel,
        grid_spec=grid_spec,
        out_shape=jax.ShapeDtypeStruct((n_rows, d // 2), jnp.uint32),
        name="moe_experts",
    )(tile_expert, n_valid, row_token, row_token, h_flat, wg, wu, wd)


def _combine_kernel(p1_ref, p2_ref, x_ref, info_ref, gate_ref, gfin_ref, y_ref, o_ref, buf1, buf2, sem1, sem2, *, final_norm):
    n = x_ref.shape[0]

    def start(r, _):
        _row_copy(y_ref, buf1, sem1, p1_ref[r], r).start()
        _row_copy(y_ref, buf2, sem2, p2_ref[r], r).start()
        return 0

    def wait(r, _):
        _row_copy(y_ref, buf1, sem1, 0, r).wait()
        _row_copy(y_ref, buf2, sem2, 0, r).wait()
        return 0

    lax.fori_loop(0, n, start, 0, unroll=DMA_UNROLL)
    lax.fori_loop(0, n, wait, 0, unroll=DMA_UNROLL)
    info = info_ref[...]
    a_hi, a_lo = _unpack_bf16_pairs(buf1[...])
    b_hi, b_lo = _unpack_bf16_pairs(buf2[...])
    w1 = info[:, 2:3]
    w2 = info[:, 3:4]
    y = jnp.concatenate([w1 * a_hi + w2 * b_hi, w1 * a_lo + w2 * b_lo], axis=1)
    out = x_ref[...] + gate_ref[0] * y
    if final_norm:
        out = _rms_rows(out) * gfin_ref[...]
    o_ref[...] = out


def _combine_call(p1, p2, x_flat, info_flat, gate_rows, g_final, y_sorted, tokens_per_batch, final_norm):
    t, d = x_flat.shape
    tc = GATHER_TILE
    per_b = tokens_per_batch // tc
    idx = pl.BlockSpec((tc,), lambda i: (i,), memory_space=pltpu.SMEM)
    return pl.pallas_call(
        functools.partial(_combine_kernel, final_norm=final_norm),
        grid=(t // tc,),
        in_specs=[idx, idx, pl.BlockSpec((tc, d), lambda i: (i, 0)), pl.BlockSpec((tc, LANES), lambda i: (i, 0)),
                  pl.BlockSpec((1, 1, d), lambda i: (i // per_b, 0, 0)), pl.BlockSpec((1, d), lambda i: (0, 0)),
                  pl.BlockSpec(memory_space=pl.ANY)],
        out_specs=pl.BlockSpec((tc, d), lambda i: (i, 0)),
        out_shape=jax.ShapeDtypeStruct((t, d), F32),
        scratch_shapes=[pltpu.VMEM((tc, d // 2), jnp.uint32), pltpu.VMEM((tc, d // 2), jnp.uint32),
                        pltpu.SemaphoreType.DMA(()), pltpu.SemaphoreType.DMA(())],
        name="moe_combine",
    )(p1, p2, x_flat, info_flat, gate_rows, g_final, y_sorted)


def _routing_tables(expert_ids, n_tokens):
    tm = EXPERT_ROW_TILE
    n_assign = n_tokens * TOP_K
    n_rows = n_assign + N_EXPERTS * tm
    e_flat = expert_ids.reshape(n_assign)
    onehot = (e_flat[:, None] == jnp.arange(N_EXPERTS, dtype=jnp.int32)[None, :]).astype(jnp.int32)
    rank = jnp.sum((jnp.cumsum(onehot, axis=0) - onehot) * onehot, axis=1)
    counts = jnp.sum(onehot, axis=0)
    padded = ((counts + tm - 1) // tm) * tm
    ends = jnp.cumsum(padded)
    starts = ends - padded
    pos = (starts[e_flat] + rank).astype(jnp.int32)
    row_token = jnp.zeros((n_rows,), jnp.int32).at[pos].set(jnp.arange(n_assign, dtype=jnp.int32) // TOP_K)
    n_valid = (ends[-1] // tm).astype(jnp.int32).reshape(1)
    tile_start = jnp.arange(n_rows // tm, dtype=jnp.int32) * tm
    tile_expert = jnp.minimum(jnp.sum((ends[None, :] <= tile_start[:, None]).astype(jnp.int32), axis=1), N_EXPERTS - 1)
    pos2 = pos.reshape(n_tokens, TOP_K)
    return row_token, tile_expert, n_valid, pos2[:, 0], pos2[:, 1]


def _prep_w_in(w_in):
    d = w_in.shape[0]
    cq, ckv, kr, qsb, ksb, vsb = jnp.split(
        w_in, (Q_LORA, Q_LORA + KV_LORA, Q_LORA + KV_LORA + MLA_ROPE,
               Q_LORA + KV_LORA + MLA_ROPE + SB_WIDTH, Q_LORA + KV_LORA + MLA_ROPE + 2 * SB_WIDTH), axis=1)
    z = lambda n: jnp.zeros((d, n), w_in.dtype)
    kr_blk = jnp.concatenate([z(MLA_NOPE), kr, z(LANES - MLA_QK_DIM)], axis=1)
    kr_swap = jnp.concatenate([z(MLA_NOPE), kr[:, HALF_ROPE:], kr[:, :HALF_ROPE], z(LANES - MLA_QK_DIM)], axis=1)
    return jnp.concatenate([cq, ckv, kr_blk, kr_swap, qsb * SB_HEAD_DIM ** -0.5, ksb, vsb], axis=1).astype(BF16)


def _prep_w_uq(w_uq):
    r = w_uq.shape[0]
    w = w_uq.reshape(r, MLA_HEADS, MLA_QK_DIM)
    nope, x1, x2 = w[..., :MLA_NOPE], w[..., MLA_NOPE:MLA_NOPE + HALF_ROPE], w[..., MLA_NOPE + HALF_ROPE:]
    pad = jnp.zeros((r, MLA_HEADS, LANES - MLA_QK_DIM), w.dtype)
    plain = jnp.concatenate([nope, x1, x2, pad], axis=-1).reshape(r, MLA_PAD_WIDTH)
    swapped = jnp.concatenate([jnp.zeros_like(nope), x2, x1, pad], axis=-1).reshape(r, MLA_PAD_WIDTH)
    return jnp.concatenate([plain, swapped], axis=1).astype(BF16)


def _prep_w_ukv(w_ukv):
    r = w_ukv.shape[0]
    w = w_ukv.reshape(r, MLA_HEADS, MLA_NOPE + MLA_V)
    k_nope, v = w[..., :MLA_NOPE], w[..., MLA_NOPE:]
    k_pad = jnp.concatenate([k_nope, jnp.zeros((r, MLA_HEADS, LANES - MLA_NOPE), w.dtype)], axis=-1)
    return jnp.concatenate([k_pad.reshape(r, MLA_PAD_WIDTH), v.reshape(r, MLA_WIDTH)], axis=1).astype(BF16)


def kernel(x, c, positions, w_ada, b_ada, g_mix, g_ffn, w_in, g_q_lat, w_uq, g_kv_lat, w_ukv, g_grp_mla, g_grp_sb, w_out, w_ff_gate, w_ff_up, w_ff_down, w_router, b_router, w_ex_gate, w_ex_up, w_ex_down, g_final):
    b, s, d = x.shape
    depth = w_ada.shape[0]
    n_tokens = b * s
    assert depth % 2 == 0, "the final norm is fused into the routed mixer of the last layer"

    c_pad = jnp.zeros((8, d), F32).at[:b].set(c)
    mod_all = _ada_call(c_pad, w_ada, b_ada)[:, :b].reshape(depth, b, N_MOD, d)

    lane = jnp.arange(LANES)
    freq = ROPE_BASE ** (-((lane - MLA_NOPE) % HALF_ROPE).astype(F32) / HALF_ROPE)
    pos_lanes = jnp.broadcast_to(positions.astype(F32)[..., None], (b, s, LANES))
    cos_t, sin_t = _rope_tables(pos_lanes, freq.reshape(1, LANES))

    row2 = lambda v: v.reshape(1, -1)
    for layer in range(depth):
        mod = mod_all[layer]
        qm, km, vtm, qs, ks, vts = _proj_call(
            x, mod, row2(g_mix[layer]), cos_t, sin_t, _prep_w_in(w_in[layer]), row2(g_q_lat[layer]),
            _prep_w_uq(w_uq[layer]), row2(g_kv_lat[layer]), _prep_w_ukv(w_ukv[layer]))
        o_mla = _mla_call(qm, km, vtm)
        o_sb = _sb_call(qs, ks, vts)
        x = _merge_call(x, o_mla, o_sb, mod, row2(g_grp_mla[layer]), row2(g_grp_sb[layer]), w_out[layer].astype(BF16))

        i = layer // 2
        if layer % 2 == 0:
            x = _ffn_call(x, mod, row2(g_ffn[layer]), w_ff_gate[i].astype(BF16), w_ff_up[i].astype(BF16),
                          w_ff_down[i].astype(BF16))
        else:
            wr = jnp.zeros((d, LANES), F32).at[:, :N_EXPERTS].set(w_router[i])
            wr_hi = wr.astype(BF16)
            wr_lo = (wr - wr_hi.astype(F32)).astype(BF16)
            br = jnp.zeros((1, LANES), F32).at[0, :N_EXPERTS].set(b_router[i])
            h, info = _router_call(x, mod, row2(g_ffn[layer]), wr_hi, wr_lo, br)
            info_flat = info.reshape(n_tokens, LANES)
            expert_ids = info_flat[:, :TOP_K].astype(jnp.int32)
            row_token, tile_expert, n_valid, p1, p2 = _routing_tables(expert_ids, n_tokens)
            ys = _expert_call(tile_expert, n_valid, row_token, h.reshape(n_tokens, d), w_ex_gate, w_ex_up, w_ex_down, i)
            final = layer == depth - 1
            x = _combine_call(p1, p2, x.reshape(n_tokens, d), info_flat, mod[:, 5:6, :], row2(g_final), ys, s,
                              final_norm=final).reshape(b, s, d)
    return x
```

```python
import functools

import jax
import jax.numpy as jnp
from jax import lax
from jax.experimental import pallas as pl
from jax.experimental.pallas import tpu as pltpu

F32 = jnp.float32
BF16 = jnp.bfloat16

EPS = 1e-6
N_MOD = 6
CHUNK = 64
MLA_HEADS = 8
MLA_NOPE = 64
MLA_ROPE = 32
MLA_V = 64
Q_LORA = 256
KV_LORA = 128
ROPE_BASE = 10000.0
SB_HEADS = 8
SB_HEAD_DIM = 64
N_EXPERTS = 8
TOP_K = 2

LANES = 128
HALF_ROPE = MLA_ROPE // 2
MLA_QK_DIM = MLA_NOPE + MLA_ROPE
LOG2E = 1.4426950408889634
NEG_BIG = -1e30

ROW_TILE = 512
ATT_TILE = 256
EXPERT_ROW_TILE = 1024
EXPERT_FF_TILE = 512
GATHER_TILE = 1024
FFN_CHUNK = 512
DMA_UNROLL = 8


def _split_bf16(v):
    hi = v.astype(BF16)
    lo = (v - hi.astype(F32)).astype(BF16)
    return hi, lo


def _dot(a, b):
    return jnp.dot(a, b, preferred_element_type=F32)


def _rms_rows(v):
    return v * lax.rsqrt(jnp.mean(v * v, axis=-1, keepdims=True) + EPS)


def _modulated_norm(x, gain, scale, shift):
    return _rms_rows(x) * (gain * (1.0 + scale)) + shift


def _ada_kernel(c_ref, w_ref, b_ref, o_ref):
    c = c_ref[...]
    act = c * (1.0 / (1.0 + jnp.exp(-c)))
    a_hi, a_lo = _split_bf16(act)
    w_hi, w_lo = _split_bf16(w_ref[0])
    o_ref[0] = _dot(a_hi, w_hi) + _dot(a_lo, w_hi) + _dot(a_hi, w_lo) + b_ref[0]


def _ada_call(c_pad, w_ada, b_ada):
    depth, d, n = w_ada.shape
    tn = n // 4
    return pl.pallas_call(
        _ada_kernel,
        grid=(depth, n // tn),
        in_specs=[
            pl.BlockSpec((8, d), lambda l, j: (0, 0)),
            pl.BlockSpec((1, d, tn), lambda l, j: (l, 0, j)),
            pl.BlockSpec((1, 1, tn), lambda l, j: (l, 0, j)),
        ],
        out_specs=pl.BlockSpec((1, 8, tn), lambda l, j: (l, 0, j)),
        out_shape=jax.ShapeDtypeStruct((depth, 8, n), F32),
        name="ada_mod",
    )(c_pad, w_ada, b_ada.reshape(depth, 1, n))


def _rope_table_kernel(pos_ref, freq_ref, cos_ref, sin_ref):
    ang = pos_ref[0] * freq_ref[...]
    lane = lax.broadcasted_iota(jnp.int32, ang.shape, 1)
    in_rope = (lane >= MLA_NOPE) & (lane < MLA_QK_DIM)
    first_half = lane < MLA_NOPE + HALF_ROPE
    cos_ref[0] = jnp.where(in_rope, jnp.cos(ang), jnp.where(lane < MLA_NOPE, 1.0, 0.0))
    s = jnp.sin(ang)
    sin_ref[0] = jnp.where(in_rope, jnp.where(first_half, -s, s), 0.0)


def _rope_tables(pos_lanes, freq_lanes):
    b, s, _ = pos_lanes.shape
    tm = ROW_TILE
    spec = pl.BlockSpec((1, tm, LANES), lambda bi, i: (bi, i, 0))
    return pl.pallas_call(
        _rope_table_kernel,
        grid=(b, s // tm),
        in_specs=[spec, pl.BlockSpec((1, LANES), lambda bi, i: (0, 0))],
        out_specs=[spec, spec],
        out_shape=[jax.ShapeDtypeStruct((b, s, LANES), F32)] * 2,
        name="rope_tables",
    )(pos_lanes, freq_lanes)


IN_OFF_CQ = 0
IN_OFF_CKV = Q_LORA
IN_OFF_KR = IN_OFF_CKV + KV_LORA
IN_OFF_KRS = IN_OFF_KR + LANES
IN_OFF_QSB = IN_OFF_KRS + LANES
SB_WIDTH = SB_HEADS * SB_HEAD_DIM
MLA_WIDTH = MLA_HEADS * MLA_V
IN_OFF_KSB = IN_OFF_QSB + SB_WIDTH
IN_OFF_VSB = IN_OFF_KSB + SB_WIDTH
IN_COLS_PADDED = IN_OFF_VSB + SB_WIDTH
MLA_PAD_WIDTH = MLA_HEADS * LANES
MLA_V_EXT = MLA_V + 16


def _proj_kernel(x_ref, mod_ref, gmix_ref, cos_ref, sin_ref, win_ref, gq_ref, wuq_ref, gkv_ref, wukv_ref,
                 qm_ref, km_ref, vtm_ref, qs_ref, ks_ref, vts_ref):
    x = x_ref[0]
    mod = mod_ref[0]
    h = _modulated_norm(x, gmix_ref[...], mod[1:2], mod[0:1])
    proj = _dot(h.astype(BF16), win_ref[...])
    cos = cos_ref[0]
    sin = sin_ref[0]
    n_kv = vtm_ref.shape[1]
    tk = vtm_ref.shape[3]

    cqn = _rms_rows(proj[:, IN_OFF_CQ:IN_OFF_CQ + Q_LORA]) * gq_ref[...]
    q2 = _dot(cqn.astype(BF16), wuq_ref[...])
    q_scale = MLA_QK_DIM ** -0.5 * LOG2E
    for hd in range(MLA_HEADS):
        lo = hd * LANES
        q = q2[:, lo:lo + LANES] * cos + q2[:, MLA_PAD_WIDTH + lo:MLA_PAD_WIDTH + lo + LANES] * sin
        qm_ref[0, :, lo:lo + LANES] = (q * q_scale).astype(qm_ref.dtype)

    ckvn = _rms_rows(proj[:, IN_OFF_CKV:IN_OFF_CKV + KV_LORA]) * gkv_ref[...]
    kv2 = _dot(ckvn.astype(BF16), wukv_ref[...])
    k_rope = proj[:, IN_OFF_KR:IN_OFF_KR + LANES] * cos + proj[:, IN_OFF_KRS:IN_OFF_KRS + LANES] * sin
    for hd in range(MLA_HEADS):
        lo = hd * LANES
        km_ref[0, :, lo:lo + LANES] = (kv2[:, lo:lo + LANES] + k_rope).astype(km_ref.dtype)
    v_t = kv2[:, MLA_PAD_WIDTH:].T
    pad_rows = MLA_V_EXT - MLA_V
    ones_row = jnp.where(lax.broadcasted_iota(jnp.int32, (pad_rows, tk), 0) == 0, 1.0, 0.0).astype(vtm_ref.dtype)
    for c in range(n_kv):
        for hd in range(MLA_HEADS):
            lo = hd * MLA_V_EXT
            vtm_ref[0, c, lo:lo + MLA_V, :] = v_t[hd * MLA_V:(hd + 1) * MLA_V, c * tk:(c + 1) * tk].astype(vtm_ref.dtype)
            vtm_ref[0, c, lo + MLA_V:lo + MLA_V_EXT, :] = ones_row

    qs_ref[0] = proj[:, IN_OFF_QSB:IN_OFF_QSB + SB_WIDTH].astype(qs_ref.dtype)
    ks_ref[0] = proj[:, IN_OFF_KSB:IN_OFF_KSB + SB_WIDTH].astype(ks_ref.dtype)
    vs_t = proj[:, IN_OFF_VSB:IN_OFF_VSB + SB_WIDTH].T
    for c in range(n_kv):
        vts_ref[0, c] = vs_t[:, c * tk:(c + 1) * tk].astype(vts_ref.dtype)


def _proj_call(x, mod, gmix, cos_t, sin_t, win, gq, wuq, gkv, wukv):
    b, s, d = x.shape
    tm = ROW_TILE
    tk = ATT_TILE
    n_kv = tm // tk
    row = lambda w: pl.BlockSpec((1, tm, w), lambda bi, i: (bi, i, 0))
    full = lambda a: pl.BlockSpec(a.shape, lambda bi, i: (0,) * a.ndim)
    vt_spec = lambda w: pl.BlockSpec((1, n_kv, w, tk), lambda bi, i: (bi, i, 0, 0))
    return pl.pallas_call(
        _proj_kernel,
        grid=(b, s // tm),
        in_specs=[row(d), pl.BlockSpec((1, N_MOD, d), lambda bi, i: (bi, 0, 0)), full(gmix), row(LANES), row(LANES),
                  full(win), full(gq), full(wuq), full(gkv), full(wukv)],
        out_specs=[row(MLA_PAD_WIDTH), row(MLA_PAD_WIDTH), vt_spec(MLA_HEADS * MLA_V_EXT),
                   row(SB_WIDTH), row(SB_WIDTH), vt_spec(SB_WIDTH)],
        out_shape=[
            jax.ShapeDtypeStruct((b, s, MLA_PAD_WIDTH), BF16),
            jax.ShapeDtypeStruct((b, s, MLA_PAD_WIDTH), BF16),
            jax.ShapeDtypeStruct((b, s // tk, MLA_HEADS * MLA_V_EXT, tk), BF16),
            jax.ShapeDtypeStruct((b, s, SB_WIDTH), BF16),
            jax.ShapeDtypeStruct((b, s, SB_WIDTH), BF16),
            jax.ShapeDtypeStruct((b, s // tk, SB_WIDTH, tk), BF16),
        ],
        name="in_proj",
    )(x, mod, gmix, cos_t, sin_t, win, gq, wuq, gkv, wukv)


def _resident(shape, index_map):
    return pl.BlockSpec(shape, index_map, pipeline_mode=pl.Buffered(1))


def _mla_kernel(q_ref, k_ref, vt_ref, o_ref, qt_scr, sc_a, sc_b, m_scr, acc_scr):
    tq = q_ref.shape[1]
    tk = vt_ref.shape[3]
    i = pl.program_id(1)
    for hd in range(MLA_HEADS):
        qt_scr[hd] = q_ref[0, :, hd * LANES:(hd + 1) * LANES].astype(F32).T.astype(BF16)
    m_scr[...] = jnp.full(m_scr.shape, NEG_BIG, F32)
    acc_scr[...] = jnp.zeros(acc_scr.shape, F32)

    def scores_into(j, sc):
        start = pl.multiple_of(j * tk, tk)
        for hd in range(MLA_HEADS):
            sc[hd] = _dot(k_ref[0, pl.ds(start, tk), hd * LANES:(hd + 1) * LANES], qt_scr[hd])

    def softmax_pv(j, sc, masked):
        probs = []
        alphas = []
        for hd in range(MLA_HEADS):
            s = sc[hd]
            if masked:
                key_chunk = lax.broadcasted_iota(jnp.int32, s.shape, 0) // CHUNK
                qry_chunk = lax.broadcasted_iota(jnp.int32, s.shape, 1) // CHUNK
                s = jnp.where(key_chunk <= qry_chunk, s, NEG_BIG)
            m = m_scr[hd]
            m_new = jnp.maximum(m, jnp.max(s, axis=0, keepdims=True))
            m_scr[hd] = m_new
            alphas.append(jnp.exp2(m - m_new))
            probs.append(jnp.exp2(s - m_new).astype(BF16))
        for hd in range(MLA_HEADS):
            pv = _dot(vt_ref[0, j, hd * MLA_V_EXT:(hd + 1) * MLA_V_EXT, :], probs[hd])
            acc_scr[hd] = alphas[hd] * acc_scr[hd] + pv

    def two_blocks(j):
        scores_into(j + 1, sc_b)
        softmax_pv(j, sc_a, masked=False)
        scores_into(j + 2, sc_a)
        softmax_pv(j + 1, sc_b, masked=False)

    def four_blocks(t, carry):
        two_blocks(4 * t)
        two_blocks(4 * t + 2)
        return carry

    scores_into(0, sc_a)
    lax.fori_loop(0, i // 4, four_blocks, 0)
    rem = i % 4

    @pl.when(rem >= 2)
    def _():
        two_blocks(i - rem)

    @pl.when(rem % 2 == 1)
    def _():
        scores_into(i, sc_b)
        softmax_pv(i - 1, sc_a, masked=False)
        softmax_pv(i, sc_b, masked=True)

    @pl.when(rem % 2 == 0)
    def _():
        softmax_pv(i, sc_a, masked=True)

    for hd in range(MLA_HEADS):
        acc = acc_scr[hd]
        o_ref[0, hd * MLA_V:(hd + 1) * MLA_V, :] = acc[:MLA_V] * (1.0 / acc[MLA_V:MLA_V + 1])


def _mla_call(q, k, vt):
    b, s, _ = q.shape
    n_kv, tk = vt.shape[1], vt.shape[3]
    tq = tk
    scores = pltpu.VMEM((MLA_HEADS, tk, tq), F32)
    return pl.pallas_call(
        _mla_kernel,
        grid=(b, s // tq),
        in_specs=[
            pl.BlockSpec((1, tq, MLA_PAD_WIDTH), lambda bi, i: (bi, i, 0)),
            _resident((1, s, MLA_PAD_WIDTH), lambda bi, i: (bi, 0, 0)),
            _resident((1, n_kv, MLA_HEADS * MLA_V_EXT, tk), lambda bi, i: (bi, 0, 0, 0)),
        ],
        out_specs=pl.BlockSpec((1, MLA_WIDTH, tq), lambda bi, i: (bi, 0, i)),
        out_shape=jax.ShapeDtypeStruct((b, MLA_WIDTH, s), F32),
        scratch_shapes=[pltpu.VMEM((MLA_HEADS, LANES, tq), BF16), scores, scores,
                        pltpu.VMEM((MLA_HEADS, 1, tq), F32), pltpu.VMEM((MLA_HEADS, MLA_V_EXT, tq), F32)],
        name="mla_attention",
    )(q, k, vt)


F32_EXP2_ZERO_BELOW = -152.0


def _sb_kernel(q_ref, k_ref, vt_ref, o_ref, qt_scr, later_scr, run_scr, acc_scr):
    tq = q_ref.shape[1]
    tk = vt_ref.shape[3]
    i = pl.program_id(1)
    for pair in range(SB_HEADS // 2):
        q_t = q_ref[0, :, pair * LANES:(pair + 1) * LANES].astype(F32).T * LOG2E
        head_row = lax.broadcasted_iota(jnp.int32, q_t.shape, 0) // SB_HEAD_DIM
        for sub in range(2):
            qt_scr[2 * pair + sub] = jnp.where(head_row == sub, q_t, 0.0).astype(BF16)
    r = lax.broadcasted_iota(jnp.int32, later_scr.shape, 0)
    c = lax.broadcasted_iota(jnp.int32, later_scr.shape, 1) % tk
    later_scr[...] = jnp.where(((r < tk) & (c > r)) | (r == tk), 1.0, 0.0).astype(BF16)
    run_scr[...] = jnp.zeros(run_scr.shape, F32)
    acc_scr[...] = jnp.zeros(acc_scr.shape, F32)

    def step(j, masked):
        start = pl.multiple_of(j * tk, tk)
        zs = [_dot(k_ref[0, pl.ds(start, tk), (hd // 2) * LANES:(hd // 2 + 1) * LANES], qt_scr[hd])
              for hd in range(SB_HEADS)]
        if masked:
            strict = lax.broadcasted_iota(jnp.int32, (tk, tq), 0) < lax.broadcasted_iota(jnp.int32, (tk, tq), 1)
        log_betas, laters = [], []
        for hd in range(SB_HEADS):
            z = zs[hd]
            soft = jnp.log2(1.0 + jnp.exp2(-jnp.abs(z)))
            drop = jnp.maximum(z, 0.0) + soft
            log_betas.append(jnp.minimum(z, 0.0) - soft)
            if masked:
                drop = jnp.where(strict, drop, 0.0)
            drop_hi, drop_lo = _split_bf16(drop)
            laters.append(_dot(later_scr[...], jnp.concatenate([drop_hi, drop_lo], axis=0)))
        weights = []
        for hd in range(SB_HEADS):
            run = run_scr[hd]
            a = jnp.exp2(log_betas[hd] - laters[hd][:tk] - run)
            if masked:
                a = jnp.where(strict, a, 0.0)
            weights.append(a.astype(BF16))
            run_scr[hd] = run + laters[hd][tk:tk + 1]
        for hd in range(SB_HEADS):
            acc_scr[hd] += _dot(vt_ref[0, j, hd * SB_HEAD_DIM:(hd + 1) * SB_HEAD_DIM, :], weights[hd])

    def any_weight_left():
        return (jnp.min(run_scr[...]) < -F32_EXP2_ZERO_BELOW).astype(jnp.int32)

    step(i, masked=True)

    def cond(carry):
        j, alive = carry
        return (j >= 0) & (alive > 0)

    def body(carry):
        j, _ = carry
        step(j, masked=False)
        return j - 1, any_weight_left()

    lax.while_loop(cond, body, (i - 1, any_weight_left()))
    for hd in range(SB_HEADS):
        o_ref[0, hd * SB_HEAD_DIM:(hd + 1) * SB_HEAD_DIM, :] = acc_scr[hd]


def _sb_call(q, k, vt):
    b, s, _ = q.shape
    n_kv, tk = vt.shape[1], vt.shape[3]
    tq = tk
    return pl.pallas_call(
        _sb_kernel,
        grid=(b, s // tq),
        in_specs=[
            pl.BlockSpec((1, tq, SB_WIDTH), lambda bi, i: (bi, i, 0)),
            _resident((1, s, SB_WIDTH), lambda bi, i: (bi, 0, 0)),
            _resident((1, n_kv, SB_WIDTH, tk), lambda bi, i: (bi, 0, 0, 0)),
        ],
        out_specs=pl.BlockSpec((1, SB_WIDTH, tq), lambda bi, i: (bi, 0, i)),
        out_shape=jax.ShapeDtypeStruct((b, SB_WIDTH, s), F32),
        scratch_shapes=[pltpu.VMEM((SB_HEADS, LANES, tq), BF16), pltpu.VMEM((tk + 16, 2 * tk), BF16),
                        pltpu.VMEM((SB_HEADS, 1, tq), F32), pltpu.VMEM((SB_HEADS, SB_HEAD_DIM, tq), F32)],
        name="sb_attention",
    )(q, k, vt)


def _merge_kernel(x_ref, om_ref, os_ref, mod_ref, gm_ref, gs_ref, wout_ref, o_ref):
    def group(o_t, gain):
        inv = lax.rsqrt(jnp.mean(o_t * o_t, axis=0, keepdims=True) + EPS)
        return ((o_t * inv).T * gain).astype(BF16)

    a = group(om_ref[0], gm_ref[...])
    bb = group(os_ref[0], gs_ref[...])
    mixed = _dot(a, wout_ref[:MLA_WIDTH, :]) + _dot(bb, wout_ref[MLA_WIDTH:, :])
    o_ref[0] = x_ref[0] + mod_ref[0][2:3] * mixed


def _merge_call(x, o_mla, o_sb, mod, g_mla, g_sb, wout):
    b, s, d = x.shape
    tm = ROW_TILE
    row = pl.BlockSpec((1, tm, d), lambda bi, i: (bi, i, 0))
    col = lambda w: pl.BlockSpec((1, w, tm), lambda bi, i: (bi, 0, i))
    full = lambda a: pl.BlockSpec(a.shape, lambda bi, i: (0,) * a.ndim)
    return pl.pallas_call(
        _merge_kernel,
        grid=(b, s // tm),
        in_specs=[row, col(MLA_WIDTH), col(SB_WIDTH), pl.BlockSpec((1, N_MOD, d), lambda bi, i: (bi, 0, 0)),
                  full(g_mla), full(g_sb), full(wout)],
        out_specs=row,
        out_shape=jax.ShapeDtypeStruct((b, s, d), F32),
        name="merge_out_proj",
    )(x, o_mla, o_sb, mod, g_mla, g_sb, wout)


def _silu(v):
    return v * (1.0 / (1.0 + jnp.exp(-v)))


def _ffn_kernel(x_ref, mod_ref, g_ref, wg_ref, wu_ref, wd_ref, o_ref):
    x = x_ref[0]
    mod = mod_ref[0]
    hb = _modulated_norm(x, g_ref[...], mod[4:5], mod[3:4]).astype(BF16)
    acc = jnp.zeros(x.shape, F32)
    for c in range(wg_ref.shape[1] // FFN_CHUNK):
        lo = c * FFN_CHUNK
        gate = _dot(hb, wg_ref[:, lo:lo + FFN_CHUNK])
        up = _dot(hb, wu_ref[:, lo:lo + FFN_CHUNK])
        acc = acc + _dot((_silu(gate) * up).astype(BF16), wd_ref[lo:lo + FFN_CHUNK, :])
    o_ref[0] = x + mod[5:6] * acc


def _ffn_call(x, mod, g, wg, wu, wd):
    b, s, d = x.shape
    tm = ROW_TILE
    row = pl.BlockSpec((1, tm, d), lambda bi, i: (bi, i, 0))
    full = lambda a: pl.BlockSpec(a.shape, lambda bi, i: (0,) * a.ndim)
    return pl.pallas_call(
        _ffn_kernel,
        grid=(b, s // tm),
        in_specs=[row, pl.BlockSpec((1, N_MOD, d), lambda bi, i: (bi, 0, 0)), full(g), full(wg), full(wu), full(wd)],
        out_specs=row,
        out_shape=jax.ShapeDtypeStruct((b, s, d), F32),
        name="dense_ffn",
    )(x, mod, g, wg, wu, wd)


def _router_kernel(x_ref, mod_ref, g_ref, wr_hi_ref, wr_lo_ref, br_ref, h_ref, info_ref):
    x = x_ref[0]
    mod = mod_ref[0]
    h = _modulated_norm(x, g_ref[...], mod[4:5], mod[3:4])
    h_ref[0] = h
    h_hi, h_lo = _split_bf16(h)
    w_hi = wr_hi_ref[...]
    logits = _dot(h_hi, w_hi) + _dot(h_lo, w_hi) + _dot(h_hi, wr_lo_ref[...]) + br_ref[...]
    lane = lax.broadcasted_iota(jnp.int32, logits.shape, 1)
    logits = jnp.where(lane < N_EXPERTS, logits, NEG_BIG)
    m1 = jnp.max(logits, axis=-1, keepdims=True)
    i1 = jnp.min(jnp.where(logits == m1, lane, LANES), axis=-1, keepdims=True)
    rest = jnp.where(lane == i1, NEG_BIG, logits)
    m2 = jnp.max(rest, axis=-1, keepdims=True)
    i2 = jnp.min(jnp.where(rest == m2, lane, LANES), axis=-1, keepdims=True)
    e = jnp.exp(m2 - m1)
    w1 = 1.0 / (1.0 + e)
    w2 = e * w1
    info = jnp.where(lane == 0, i1.astype(F32),
                     jnp.where(lane == 1, i2.astype(F32), jnp.where(lane == 2, w1, jnp.where(lane == 3, w2, 0.0))))
    info_ref[0] = info


def _router_call(x, mod, g, wr_hi, wr_lo, br):
    b, s, d = x.shape
    tm = ROW_TILE
    row = pl.BlockSpec((1, tm, d), lambda bi, i: (bi, i, 0))
    full = lambda a: pl.BlockSpec(a.shape, lambda bi, i: (0,) * a.ndim)
    return pl.pallas_call(
        _router_kernel,
        grid=(b, s // tm),
        in_specs=[row, pl.BlockSpec((1, N_MOD, d), lambda bi, i: (bi, 0, 0)), full(g), full(wr_hi), full(wr_lo), full(br)],
        out_specs=[row, pl.BlockSpec((1, tm, LANES), lambda bi, i: (bi, i, 0))],
        out_shape=[jax.ShapeDtypeStruct((b, s, d), F32), jax.ShapeDtypeStruct((b, s, LANES), F32)],
        name="moe_router",
    )(x, mod, g, wr_hi, wr_lo, br)


def _row_copy(src_hbm, dst_vmem, sem, src_row, dst_row):
    return pltpu.make_async_copy(src_hbm.at[pl.ds(src_row, 1), :], dst_vmem.at[pl.ds(dst_row, 1), :], sem)


def _start_rows(idx_ref, src_hbm, dst_vmem, sem, n_idx):
    def body(r, carry):
        _row_copy(src_hbm, dst_vmem, sem, idx_ref[jnp.minimum(r, n_idx - 1)], r).start()
        return carry

    lax.fori_loop(0, dst_vmem.shape[0], body, 0, unroll=DMA_UNROLL)


def _wait_rows(src_hbm, dst_vmem, sem):
    def body(r, carry):
        _row_copy(src_hbm, dst_vmem, sem, 0, r).wait()
        return carry

    lax.fori_loop(0, dst_vmem.shape[0], body, 0, unroll=DMA_UNROLL)


def _pack_bf16_pairs(v):
    w = v.shape[1] // 2
    hi = lax.bitcast_convert_type(v[:, :w].astype(BF16).astype(F32), jnp.uint32)
    lo = lax.bitcast_convert_type(v[:, w:].astype(BF16).astype(F32), jnp.uint32)
    return hi | (lo >> 16)


def _unpack_bf16_pairs(p):
    hi = lax.bitcast_convert_type(p & jnp.uint32(0xFFFF0000), F32)
    lo = lax.bitcast_convert_type(p << 16, F32)
    return hi, lo


def _expert_kernel(tile_expert_ref, n_valid_ref, idx_ref, idx_next_ref, h_ref, wg_ref, wu_ref, wd_ref, o_ref,
                   rows_scr, xb_scr, acc_scr, sem, *, n_steps):
    i = pl.program_id(0)
    j = pl.program_id(1)
    n_valid = n_valid_ref[0]
    tm = xb_scr.shape[0]
    per_step = rows_scr.shape[0] // n_steps

    @pl.when((i == 0) & (j == 0))
    def _():
        _start_rows(idx_ref, h_ref, rows_scr, sem, tm)

    @pl.when((i < n_valid) & (j == 0))
    def _():
        _wait_rows(h_ref, rows_scr, sem)
        xb_scr[...] = rows_scr[:tm].astype(BF16)
        acc_scr[...] = jnp.zeros(acc_scr.shape, F32)

    @pl.when(i < n_valid)
    def _():
        xb = xb_scr[...]
        gate = _dot(xb, wg_ref[0, 0].astype(BF16))
        up = _dot(xb, wu_ref[0, 0].astype(BF16))
        acc_scr[...] += _dot((_silu(gate) * up).astype(BF16), wd_ref[0, 0].astype(BF16))
        for u in range(per_step):
            r = j * per_step + u
            _row_copy(h_ref, rows_scr, sem, idx_next_ref[jnp.minimum(r, tm - 1)], r).start()

    @pl.when((i < n_valid) & (j == n_steps - 1))
    def _():
        o_ref[...] = _pack_bf16_pairs(acc_scr[...])

    @pl.when((i == n_valid - 1) & (j == n_steps - 1))
    def _():
        _wait_rows(h_ref, rows_scr, sem)

    @pl.when((i >= n_valid) & (j == 0))
    def _():
        o_ref[...] = jnp.zeros(o_ref.shape, o_ref.dtype)


def _expert_call(tile_expert, n_valid, row_token, h_flat, wg, wu, wd, layer_idx):
    n_rows = row_token.shape[0]
    d = h_flat.shape[1]
    tm = EXPERT_ROW_TILE
    tf = EXPERT_FF_TILE
    n_ff = wg.shape[-1]
    n_tiles = n_rows // tm
    n_steps = n_ff // tf
    last_chunk = n_steps - 1
    rows_buf = -(-tm // (n_steps * DMA_UNROLL)) * n_steps * DMA_UNROLL

    def tile(i, nv):
        return jnp.minimum(i, nv[0] - 1)

    def chunk(i, j, nv):
        return jnp.where(i < nv[0], j, last_chunk)

    grid_spec = pltpu.PrefetchScalarGridSpec(
        num_scalar_prefetch=2,
        grid=(n_tiles, n_steps),
        in_specs=[
            pl.BlockSpec((tm,), lambda i, j, te, nv: (tile(i, nv),), memory_space=pltpu.SMEM),
            pl.BlockSpec((tm,), lambda i, j, te, nv: (tile(i + 1, nv),), memory_space=pltpu.SMEM),
            pl.BlockSpec(memory_space=pl.ANY),
            pl.BlockSpec((1, 1, d, tf), lambda i, j, te, nv: (layer_idx, te[tile(i, nv)], 0, chunk(i, j, nv))),
            pl.BlockSpec((1, 1, d, tf), lambda i, j, te, nv: (layer_idx, te[tile(i, nv)], 0, chunk(i, j, nv))),
            pl.BlockSpec((1, 1, tf, d), lambda i, j, te, nv: (layer_idx, te[tile(i, nv)], chunk(i, j, nv), 0)),
        ],
        out_specs=pl.BlockSpec((tm, d // 2), lambda i, j, te, nv: (i, 0)),
        scratch_shapes=[pltpu.VMEM((rows_buf, d), F32), pltpu.VMEM((tm, d), BF16), pltpu.VMEM((tm, d), F32),
                        pltpu.SemaphoreType.DMA(())],
    )
    return pl.pallas_call(
        functools.partial(_expert_kernel, n_steps=n_steps),
        grid_spec=grid_spec,
        out_shape=jax.ShapeDtypeStruct((n_rows, d // 2), jnp.uint32),
        name="moe_experts",
    )(tile_expert, n_valid, row_token, row_token, h_flat, wg, wu, wd)


def _combine_kernel(p1_ref, p2_ref, x_ref, info_ref, gate_ref, gfin_ref, y_ref, o_ref, buf1, buf2, sem1, sem2, *, final_norm):
    n = x_ref.shape[0]

    def start(r, _):
        _row_copy(y_ref, buf1, sem1, p1_ref[r], r).start()
        _row_copy(y_ref, buf2, sem2, p2_ref[r], r).start()
        return 0

    def wait(r, _):
        _row_copy(y_ref, buf1, sem1, 0, r).wait()
        _row_copy(y_ref, buf2, sem2, 0, r).wait()
        return 0

    lax.fori_loop(0, n, start, 0, unroll=DMA_UNROLL)
    lax.fori_loop(0, n, wait, 0, unroll=DMA_UNROLL)
    info = info_ref[...]
    a_hi, a_lo = _unpack_bf16_pairs(buf1[...])
    b_hi, b_lo = _unpack_bf16_pairs(buf2[...])
    w1 = info[:, 2:3]
    w2 = info[:, 3:4]
    y = jnp.concatenate([w1 * a_hi + w2 * b_hi, w1 * a_lo + w2 * b_lo], axis=1)
    out = x_ref[...] + gate_ref[0] * y
    if final_norm:
        out = _rms_rows(out) * gfin_ref[...]
    o_ref[...] = out


def _combine_call(p1, p2, x_flat, info_flat, gate_rows, g_final, y_sorted, tokens_per_batch, final_norm):
    t, d = x_flat.shape
    tc = GATHER_TILE
    per_b = tokens_per_batch // tc
    idx = pl.BlockSpec((tc,), lambda i: (i,), memory_space=pltpu.SMEM)
    return pl.pallas_call(
        functools.partial(_combine_kernel, final_norm=final_norm),
        grid=(t // tc,),
        in_specs=[idx, idx, pl.BlockSpec((tc, d), lambda i: (i, 0)), pl.BlockSpec((tc, LANES), lambda i: (i, 0)),
                  pl.BlockSpec((1, 1, d), lambda i: (i // per_b, 0, 0)), pl.BlockSpec((1, d), lambda i: (0, 0)),
                  pl.BlockSpec(memory_space=pl.ANY)],
        out_specs=pl.BlockSpec((tc, d), lambda i: (i, 0)),
        out_shape=jax.ShapeDtypeStruct((t, d), F32),
        scratch_shapes=[pltpu.VMEM((tc, d // 2), jnp.uint32), pltpu.VMEM((tc, d // 2), jnp.uint32),
                        pltpu.SemaphoreType.DMA(()), pltpu.SemaphoreType.DMA(())],
        name="moe_combine",
    )(p1, p2, x_flat, info_flat, gate_rows, g_final, y_sorted)


def _routing_tables(expert_ids, n_tokens):
    tm = EXPERT_ROW_TILE
    n_assign = n_tokens * TOP_K
    n_rows = n_assign + N_EXPERTS * tm
    e_flat = expert_ids.reshape(n_assign)
    onehot = (e_flat[:, None] == jnp.arange(N_EXPERTS, dtype=jnp.int32)[None, :]).astype(jnp.int32)
    rank = jnp.sum((jnp.cumsum(onehot, axis=0) - onehot) * onehot, axis=1)
    counts = jnp.sum(onehot, axis=0)
    padded = ((counts + tm - 1) // tm) * tm
    ends = jnp.cumsum(padded)
    starts = ends - padded
    pos = (starts[e_flat] + rank).astype(jnp.int32)
    row_token = jnp.zeros((n_rows,), jnp.int32).at[pos].set(jnp.arange(n_assign, dtype=jnp.int32) // TOP_K)
    n_valid = (ends[-1] // tm).astype(jnp.int32).reshape(1)
    tile_start = jnp.arange(n_rows // tm, dtype=jnp.int32) * tm
    tile_expert = jnp.minimum(jnp.sum((ends[None, :] <= tile_start[:, None]).astype(jnp.int32), axis=1), N_EXPERTS - 1)
    pos2 = pos.reshape(n_tokens, TOP_K)
    return row_token, tile_expert, n_valid, pos2[:, 0], pos2[:, 1]


def _prep_w_in(w_in):
    d = w_in.shape[0]
    cq, ckv, kr, qsb, ksb, vsb = jnp.split(
        w_in, (Q_LORA, Q_LORA + KV_LORA, Q_LORA + KV_LORA + MLA_ROPE,
               Q_LORA + KV_LORA + MLA_ROPE + SB_WIDTH, Q_LORA + KV_LORA + MLA_ROPE + 2 * SB_WIDTH), axis=1)
    z = lambda n: jnp.zeros((d, n), w_in.dtype)
    kr_blk = jnp.concatenate([z(MLA_NOPE), kr, z(LANES - MLA_QK_DIM)], axis=1)
    kr_swap = jnp.concatenate([z(MLA_NOPE), kr[:, HALF_ROPE:], kr[:, :HALF_ROPE], z(LANES - MLA_QK_DIM)], axis=1)
    return jnp.concatenate([cq, ckv, kr_blk, kr_swap, qsb * SB_HEAD_DIM ** -0.5, ksb, vsb], axis=1).astype(BF16)


def _prep_w_uq(w_uq):
    r = w_uq.shape[0]
    w = w_uq.reshape(r, MLA_HEADS, MLA_QK_DIM)
    nope, x1, x2 = w[..., :MLA_NOPE], w[..., MLA_NOPE:MLA_NOPE + HALF_ROPE], w[..., MLA_NOPE + HALF_ROPE:]
    pad = jnp.zeros((r, MLA_HEADS, LANES - MLA_QK_DIM), w.dtype)
    plain = jnp.concatenate([nope, x1, x2, pad], axis=-1).reshape(r, MLA_PAD_WIDTH)
    swapped = jnp.concatenate([jnp.zeros_like(nope), x2, x1, pad], axis=-1).reshape(r, MLA_PAD_WIDTH)
    return jnp.concatenate([plain, swapped], axis=1).astype(BF16)


def _prep_w_ukv(w_ukv):
    r = w_ukv.shape[0]
    w = w_ukv.reshape(r, MLA_HEADS, MLA_NOPE + MLA_V)
    k_nope, v = w[..., :MLA_NOPE], w[..., MLA_NOPE:]
    k_pad = jnp.concatenate([k_nope, jnp.zeros((r, MLA_HEADS, LANES - MLA_NOPE), w.dtype)], axis=-1)
    return jnp.concatenate([k_pad.reshape(r, MLA_PAD_WIDTH), v.reshape(r, MLA_WIDTH)], axis=1).astype(BF16)


def kernel(x, c, positions, w_ada, b_ada, g_mix, g_ffn, w_in, g_q_lat, w_uq, g_kv_lat, w_ukv, g_grp_mla, g_grp_sb, w_out, w_ff_gate, w_ff_up, w_ff_down, w_router, b_router, w_ex_gate, w_ex_up, w_ex_down, g_final):
    b, s, d = x.shape
    depth = w_ada.shape[0]
    n_tokens = b * s
    assert depth % 2 == 0, "the final norm is fused into the routed mixer of the last layer"

    c_pad = jnp.zeros((8, d), F32).at[:b].set(c)
    mod_all = _ada_call(c_pad, w_ada, b_ada)[:, :b].reshape(depth, b, N_MOD, d)

    lane = jnp.arange(LANES)
    freq = ROPE_BASE ** (-((lane - MLA_NOPE) % HALF_ROPE).astype(F32) / HALF_ROPE)
    pos_lanes = jnp.broadcast_to(positions.astype(F32)[..., None], (b, s, LANES))
    cos_t, sin_t = _rope_tables(pos_lanes, freq.reshape(1, LANES))

    row2 = lambda v: v.reshape(1, -1)
    for layer in range(depth):
        mod = mod_all[layer]
        qm, km, vtm, qs, ks, vts = _proj_call(
            x, mod, row2(g_mix[layer]), cos_t, sin_t, _prep_w_in(w_in[layer]), row2(g_q_lat[layer]),
            _prep_w_uq(w_uq[layer]), row2(g_kv_lat[layer]), _prep_w_ukv(w_ukv[layer]))
        o_mla = _mla_call(qm, km, vtm)
        o_sb = _sb_call(qs, ks, vts)
        x = _merge_call(x, o_mla, o_sb, mod, row2(g_grp_mla[layer]), row2(g_grp_sb[layer]), w_out[layer].astype(BF16))

        i = layer // 2
        if layer % 2 == 0:
            x = _ffn_call(x, mod, row2(g_ffn[layer]), w_ff_gate[i].astype(BF16), w_ff_up[i].astype(BF16),
                          w_ff_down[i].astype(BF16))
        else:
            wr = jnp.zeros((d, LANES), F32).at[:, :N_EXPERTS].set(w_router[i])
            wr_hi = wr.astype(BF16)
            wr_lo = (wr - wr_hi.astype(F32)).astype(BF16)
            br = jnp.zeros((1, LANES), F32).at[0, :N_EXPERTS].set(b_router[i])
            h, info = _router_call(x, mod, row2(g_ffn[layer]), wr_hi, wr_lo, br)
            info_flat = info.reshape(n_tokens, LANES)
            expert_ids = info_flat[:, :TOP_K].astype(jnp.int32)
            row_token, tile_expert, n_valid, p1, p2 = _routing_tables(expert_ids, n_tokens)
            ys = _expert_call(tile_expert, n_valid, row_token, h.reshape(n_tokens, d), w_ex_gate, w_ex_up, w_ex_down, i)
            final = layer == depth - 1
            x = _combine_call(p1, p2, x.reshape(n_tokens, d), info_flat, mod[:, 5:6, :], row2(g_final), ys, s,
                              final_norm=final).reshape(b, s, d)
    return x
```

```python
import functools

import jax
import jax.numpy as jnp
from jax import lax
from jax.experimental import pallas as pl
from jax.experimental.pallas import tpu as pltpu

F32 = jnp.float32
BF16 = jnp.bfloat16

EPS = 1e-6
N_MOD = 6
CHUNK = 64
MLA_HEADS = 8
MLA_NOPE = 64
MLA_ROPE = 32
MLA_V = 64
Q_LORA = 256
KV_LORA = 128
ROPE_BASE = 10000.0
SB_HEADS = 8
SB_HEAD_DIM = 64
N_EXPERTS = 8
TOP_K = 2

LANES = 128
HALF_ROPE = MLA_ROPE // 2
MLA_QK_DIM = MLA_NOPE + MLA_ROPE
LOG2E = 1.4426950408889634
NEG_BIG = -1e30

ROW_TILE = 512
ATT_TILE = 256
EXPERT_ROW_TILE = 1024
EXPERT_FF_TILE = 512
GATHER_TILE = 1024
FFN_CHUNK = 512
DMA_UNROLL = 8
ROW_COPY_PRIORITY = 1


def _split_bf16(v):
    hi = v.astype(BF16)
    lo = (v - hi.astype(F32)).astype(BF16)
    return hi, lo


def _dot(a, b):
    return jnp.dot(a, b, preferred_element_type=F32)


def _rms_rows(v):
    return v * lax.rsqrt(jnp.mean(v * v, axis=-1, keepdims=True) + EPS)


def _modulated_norm(x, gain, scale, shift):
    return _rms_rows(x) * (gain * (1.0 + scale)) + shift


def _ada_kernel(c_ref, w_ref, b_ref, o_ref):
    c = c_ref[...]
    act = c * (1.0 / (1.0 + jnp.exp(-c)))
    a_hi, a_lo = _split_bf16(act)
    w_hi, w_lo = _split_bf16(w_ref[0])
    o_ref[0] = _dot(a_hi, w_hi) + _dot(a_lo, w_hi) + _dot(a_hi, w_lo) + b_ref[0]


def _ada_call(c_pad, w_ada, b_ada):
    depth, d, n = w_ada.shape
    tn = n // 4
    return pl.pallas_call(
        _ada_kernel,
        grid=(depth, n // tn),
        in_specs=[
            pl.BlockSpec((8, d), lambda l, j: (0, 0)),
            pl.BlockSpec((1, d, tn), lambda l, j: (l, 0, j)),
            pl.BlockSpec((1, 1, tn), lambda l, j: (l, 0, j)),
        ],
        out_specs=pl.BlockSpec((1, 8, tn), lambda l, j: (l, 0, j)),
        out_shape=jax.ShapeDtypeStruct((depth, 8, n), F32),
        name="ada_mod",
    )(c_pad, w_ada, b_ada.reshape(depth, 1, n))


def _rope_table_kernel(pos_ref, freq_ref, cos_ref, sin_ref):
    ang = pos_ref[0] * freq_ref[...]
    lane = lax.broadcasted_iota(jnp.int32, ang.shape, 1)
    in_rope = (lane >= MLA_NOPE) & (lane < MLA_QK_DIM)
    first_half = lane < MLA_NOPE + HALF_ROPE
    cos_ref[0] = jnp.where(in_rope, jnp.cos(ang), jnp.where(lane < MLA_NOPE, 1.0, 0.0))
    s = jnp.sin(ang)
    sin_ref[0] = jnp.where(in_rope, jnp.where(first_half, -s, s), 0.0)


def _rope_tables(pos_lanes, freq_lanes):
    b, s, _ = pos_lanes.shape
    tm = ROW_TILE
    spec = pl.BlockSpec((1, tm, LANES), lambda bi, i: (bi, i, 0))
    return pl.pallas_call(
        _rope_table_kernel,
        grid=(b, s // tm),
        in_specs=[spec, pl.BlockSpec((1, LANES), lambda bi, i: (0, 0))],
        out_specs=[spec, spec],
        out_shape=[jax.ShapeDtypeStruct((b, s, LANES), F32)] * 2,
        name="rope_tables",
    )(pos_lanes, freq_lanes)


IN_OFF_CQ = 0
IN_OFF_CKV = Q_LORA
IN_OFF_KR = IN_OFF_CKV + KV_LORA
IN_OFF_KRS = IN_OFF_KR + LANES
IN_OFF_QSB = IN_OFF_KRS + LANES
SB_WIDTH = SB_HEADS * SB_HEAD_DIM
MLA_WIDTH = MLA_HEADS * MLA_V
IN_OFF_KSB = IN_OFF_QSB + SB_WIDTH
IN_OFF_VSB = IN_OFF_KSB + SB_WIDTH
IN_COLS_PADDED = IN_OFF_VSB + SB_WIDTH
MLA_PAD_WIDTH = MLA_HEADS * LANES
MLA_V_EXT = MLA_V + 16


def _proj_kernel(x_ref, mod_ref, gmix_ref, cos_ref, sin_ref, win_ref, gq_ref, wuq_ref, gkv_ref, wukv_ref,
                 qm_ref, km_ref, vtm_ref, qs_ref, ks_ref, vts_ref):
    x = x_ref[0]
    mod = mod_ref[0]
    h = _modulated_norm(x, gmix_ref[...], mod[1:2], mod[0:1])
    proj = _dot(h.astype(BF16), win_ref[...])
    cos = cos_ref[0]
    sin = sin_ref[0]
    n_kv = vtm_ref.shape[1]
    tk = vtm_ref.shape[3]

    cqn = _rms_rows(proj[:, IN_OFF_CQ:IN_OFF_CQ + Q_LORA]) * gq_ref[...]
    q2 = _dot(cqn.astype(BF16), wuq_ref[...])
    q_scale = MLA_QK_DIM ** -0.5 * LOG2E
    for hd in range(MLA_HEADS):
        lo = hd * LANES
        q = q2[:, lo:lo + LANES] * cos + q2[:, MLA_PAD_WIDTH + lo:MLA_PAD_WIDTH + lo + LANES] * sin
        qm_ref[0, :, lo:lo + LANES] = (q * q_scale).astype(qm_ref.dtype)

    ckvn = _rms_rows(proj[:, IN_OFF_CKV:IN_OFF_CKV + KV_LORA]) * gkv_ref[...]
    kv2 = _dot(ckvn.astype(BF16), wukv_ref[...])
    k_rope = proj[:, IN_OFF_KR:IN_OFF_KR + LANES] * cos + proj[:, IN_OFF_KRS:IN_OFF_KRS + LANES] * sin
    for hd in range(MLA_HEADS):
        lo = hd * LANES
        km_ref[0, :, lo:lo + LANES] = (kv2[:, lo:lo + LANES] + k_rope).astype(km_ref.dtype)
    v_t = kv2[:, MLA_PAD_WIDTH:].T
    pad_rows = MLA_V_EXT - MLA_V
    ones_row = jnp.where(lax.broadcasted_iota(jnp.int32, (pad_rows, tk), 0) == 0, 1.0, 0.0).astype(vtm_ref.dtype)
    for c in range(n_kv):
        for hd in range(MLA_HEADS):
            lo = hd * MLA_V_EXT
            vtm_ref[0, c, lo:lo + MLA_V, :] = v_t[hd * MLA_V:(hd + 1) * MLA_V, c * tk:(c + 1) * tk].astype(vtm_ref.dtype)
            vtm_ref[0, c, lo + MLA_V:lo + MLA_V_EXT, :] = ones_row

    qs_ref[0] = proj[:, IN_OFF_QSB:IN_OFF_QSB + SB_WIDTH].astype(qs_ref.dtype)
    ks_ref[0] = proj[:, IN_OFF_KSB:IN_OFF_KSB + SB_WIDTH].astype(ks_ref.dtype)
    vs_t = proj[:, IN_OFF_VSB:IN_OFF_VSB + SB_WIDTH].T
    for c in range(n_kv):
        vts_ref[0, c] = vs_t[:, c * tk:(c + 1) * tk].astype(vts_ref.dtype)


def _proj_call(x, mod, gmix, cos_t, sin_t, win, gq, wuq, gkv, wukv):
    b, s, d = x.shape
    tm = ROW_TILE
    tk = ATT_TILE
    n_kv = tm // tk
    row = lambda w: pl.BlockSpec((1, tm, w), lambda bi, i: (bi, i, 0))
    full = lambda a: pl.BlockSpec(a.shape, lambda bi, i: (0,) * a.ndim)
    vt_spec = lambda w: pl.BlockSpec((1, n_kv, w, tk), lambda bi, i: (bi, i, 0, 0))
    return pl.pallas_call(
        _proj_kernel,
        grid=(b, s // tm),
        in_specs=[row(d), pl.BlockSpec((1, N_MOD, d), lambda bi, i: (bi, 0, 0)), full(gmix), row(LANES), row(LANES),
                  full(win), full(gq), full(wuq), full(gkv), full(wukv)],
        out_specs=[row(MLA_PAD_WIDTH), row(MLA_PAD_WIDTH), vt_spec(MLA_HEADS * MLA_V_EXT),
                   row(SB_WIDTH), row(SB_WIDTH), vt_spec(SB_WIDTH)],
        out_shape=[
            jax.ShapeDtypeStruct((b, s, MLA_PAD_WIDTH), BF16),
            jax.ShapeDtypeStruct((b, s, MLA_PAD_WIDTH), BF16),
            jax.ShapeDtypeStruct((b, s // tk, MLA_HEADS * MLA_V_EXT, tk), BF16),
            jax.ShapeDtypeStruct((b, s, SB_WIDTH), BF16),
            jax.ShapeDtypeStruct((b, s, SB_WIDTH), BF16),
            jax.ShapeDtypeStruct((b, s // tk, SB_WIDTH, tk), BF16),
        ],
        name="in_proj",
    )(x, mod, gmix, cos_t, sin_t, win, gq, wuq, gkv, wukv)


def _resident(shape, index_map):
    return pl.BlockSpec(shape, index_map, pipeline_mode=pl.Buffered(1))


def _mla_kernel(q_ref, k_ref, vt_ref, o_ref, qt_scr, sc_a, sc_b, m_scr, acc_scr):
    tq = q_ref.shape[1]
    tk = vt_ref.shape[3]
    i = pl.program_id(1)
    for hd in range(MLA_HEADS):
        qt_scr[hd] = q_ref[0, :, hd * LANES:(hd + 1) * LANES].astype(F32).T.astype(BF16)
    m_scr[...] = jnp.full(m_scr.shape, NEG_BIG, F32)
    acc_scr[...] = jnp.zeros(acc_scr.shape, F32)

    def scores_into(j, sc):
        start = pl.multiple_of(j * tk, tk)
        for hd in range(MLA_HEADS):
            sc[hd] = _dot(k_ref[0, pl.ds(start, tk), hd * LANES:(hd + 1) * LANES], qt_scr[hd])

    def softmax_pv(j, sc, masked):
        probs = []
        alphas = []
        for hd in range(MLA_HEADS):
            s = sc[hd]
            if masked:
                key_chunk = lax.broadcasted_iota(jnp.int32, s.shape, 0) // CHUNK
                qry_chunk = lax.broadcasted_iota(jnp.int32, s.shape, 1) // CHUNK
                s = jnp.where(key_chunk <= qry_chunk, s, NEG_BIG)
            m = m_scr[hd]
            m_new = jnp.maximum(m, jnp.max(s, axis=0, keepdims=True))
            m_scr[hd] = m_new
            alphas.append(jnp.exp2(m - m_new))
            probs.append(jnp.exp2(s - m_new).astype(BF16))
        for hd in range(MLA_HEADS):
            pv = _dot(vt_ref[0, j, hd * MLA_V_EXT:(hd + 1) * MLA_V_EXT, :], probs[hd])
            acc_scr[hd] = alphas[hd] * acc_scr[hd] + pv

    def two_blocks(j):
        scores_into(j + 1, sc_b)
        softmax_pv(j, sc_a, masked=False)
        scores_into(j + 2, sc_a)
        softmax_pv(j + 1, sc_b, masked=False)

    def four_blocks(t, carry):
        two_blocks(4 * t)
        two_blocks(4 * t + 2)
        return carry

    scores_into(0, sc_a)
    lax.fori_loop(0, i // 4, four_blocks, 0)
    rem = i % 4

    @pl.when(rem >= 2)
    def _():
        two_blocks(i - rem)

    @pl.when(rem % 2 == 1)
    def _():
        scores_into(i, sc_b)
        softmax_pv(i - 1, sc_a, masked=False)
        softmax_pv(i, sc_b, masked=True)

    @pl.when(rem % 2 == 0)
    def _():
        softmax_pv(i, sc_a, masked=True)

    for hd in range(MLA_HEADS):
        acc = acc_scr[hd]
        o_ref[0, hd * MLA_V:(hd + 1) * MLA_V, :] = acc[:MLA_V] * (1.0 / acc[MLA_V:MLA_V + 1])


def _mla_call(q, k, vt):
    b, s, _ = q.shape
    n_kv, tk = vt.shape[1], vt.shape[3]
    tq = tk
    scores = pltpu.VMEM((MLA_HEADS, tk, tq), F32)
    return pl.pallas_call(
        _mla_kernel,
        grid=(b, s // tq),
        in_specs=[
            pl.BlockSpec((1, tq, MLA_PAD_WIDTH), lambda bi, i: (bi, i, 0)),
            _resident((1, s, MLA_PAD_WIDTH), lambda bi, i: (bi, 0, 0)),
            _resident((1, n_kv, MLA_HEADS * MLA_V_EXT, tk), lambda bi, i: (bi, 0, 0, 0)),
        ],
        out_specs=pl.BlockSpec((1, MLA_WIDTH, tq), lambda bi, i: (bi, 0, i)),
        out_shape=jax.ShapeDtypeStruct((b, MLA_WIDTH, s), F32),
        scratch_shapes=[pltpu.VMEM((MLA_HEADS, LANES, tq), BF16), scores, scores,
                        pltpu.VMEM((MLA_HEADS, 1, tq), F32), pltpu.VMEM((MLA_HEADS, MLA_V_EXT, tq), F32)],
        name="mla_attention",
    )(q, k, vt)


F32_EXP2_ZERO_BELOW = -152.0


def _sb_kernel(q_ref, k_ref, vt_ref, o_ref, qt_scr, later_scr, run_scr, acc_scr):
    tq = q_ref.shape[1]
    tk = vt_ref.shape[3]
    i = pl.program_id(1)
    for pair in range(SB_HEADS // 2):
        q_t = q_ref[0, :, pair * LANES:(pair + 1) * LANES].astype(F32).T * LOG2E
        head_row = lax.broadcasted_iota(jnp.int32, q_t.shape, 0) // SB_HEAD_DIM
        for sub in range(2):
            qt_scr[2 * pair + sub] = jnp.where(head_row == sub, q_t, 0.0).astype(BF16)
    r = lax.broadcasted_iota(jnp.int32, later_scr.shape, 0)
    c = lax.broadcasted_iota(jnp.int32, later_scr.shape, 1) % tk
    later_scr[...] = jnp.where(((r < tk) & (c > r)) | (r == tk), 1.0, 0.0).astype(BF16)
    run_scr[...] = jnp.zeros(run_scr.shape, F32)
    acc_scr[...] = jnp.zeros(acc_scr.shape, F32)

    def step(j, masked):
        start = pl.multiple_of(j * tk, tk)
        zs = [_dot(k_ref[0, pl.ds(start, tk), (hd // 2) * LANES:(hd // 2 + 1) * LANES], qt_scr[hd])
              for hd in range(SB_HEADS)]
        if masked:
            strict = lax.broadcasted_iota(jnp.int32, (tk, tq), 0) < lax.broadcasted_iota(jnp.int32, (tk, tq), 1)
        log_betas, laters = [], []
        for hd in range(SB_HEADS):
            z = zs[hd]
            soft = jnp.log2(1.0 + jnp.exp2(-jnp.abs(z)))
            drop = jnp.maximum(z, 0.0) + soft
            log_betas.append(jnp.minimum(z, 0.0) - soft)
            if masked:
                drop = jnp.where(strict, drop, 0.0)
            drop_hi, drop_lo = _split_bf16(drop)
            laters.append(_dot(later_scr[...], jnp.concatenate([drop_hi, drop_lo], axis=0)))
        weights = []
        for hd in range(SB_HEADS):
            run = run_scr[hd]
            a = jnp.exp2(log_betas[hd] - laters[hd][:tk] - run)
            if masked:
                a = jnp.where(strict, a, 0.0)
            weights.append(a.astype(BF16))
            run_scr[hd] = run + laters[hd][tk:tk + 1]
        for hd in range(SB_HEADS):
            acc_scr[hd] += _dot(vt_ref[0, j, hd * SB_HEAD_DIM:(hd + 1) * SB_HEAD_DIM, :], weights[hd])

    def any_weight_left():
        return (jnp.min(run_scr[...]) < -F32_EXP2_ZERO_BELOW).astype(jnp.int32)

    step(i, masked=True)

    def cond(carry):
        j, alive = carry
        return (j >= 0) & (alive > 0)

    def body(carry):
        j, _ = carry
        step(j, masked=False)
        return j - 1, any_weight_left()

    lax.while_loop(cond, body, (i - 1, any_weight_left()))
    for hd in range(SB_HEADS):
        o_ref[0, hd * SB_HEAD_DIM:(hd + 1) * SB_HEAD_DIM, :] = acc_scr[hd]


def _sb_call(q, k, vt):
    b, s, _ = q.shape
    n_kv, tk = vt.shape[1], vt.shape[3]
    tq = tk
    return pl.pallas_call(
        _sb_kernel,
        grid=(b, s // tq),
        in_specs=[
            pl.BlockSpec((1, tq, SB_WIDTH), lambda bi, i: (bi, i, 0)),
            _resident((1, s, SB_WIDTH), lambda bi, i: (bi, 0, 0)),
            _resident((1, n_kv, SB_WIDTH, tk), lambda bi, i: (bi, 0, 0, 0)),
        ],
        out_specs=pl.BlockSpec((1, SB_WIDTH, tq), lambda bi, i: (bi, 0, i)),
        out_shape=jax.ShapeDtypeStruct((b, SB_WIDTH, s), F32),
        scratch_shapes=[pltpu.VMEM((SB_HEADS, LANES, tq), BF16), pltpu.VMEM((tk + 16, 2 * tk), BF16),
                        pltpu.VMEM((SB_HEADS, 1, tq), F32), pltpu.VMEM((SB_HEADS, SB_HEAD_DIM, tq), F32)],
        name="sb_attention",
    )(q, k, vt)


def _merge_kernel(x_ref, om_ref, os_ref, mod_ref, gm_ref, gs_ref, wout_ref, o_ref):
    def group(o_t, gain):
        inv = lax.rsqrt(jnp.mean(o_t * o_t, axis=0, keepdims=True) + EPS)
        return ((o_t * inv).T * gain).astype(BF16)

    a = group(om_ref[0], gm_ref[...])
    bb = group(os_ref[0], gs_ref[...])
    mixed = _dot(a, wout_ref[:MLA_WIDTH, :]) + _dot(bb, wout_ref[MLA_WIDTH:, :])
    o_ref[0] = x_ref[0] + mod_ref[0][2:3] * mixed


def _merge_call(x, o_mla, o_sb, mod, g_mla, g_sb, wout):
    b, s, d = x.shape
    tm = ROW_TILE
    row = pl.BlockSpec((1, tm, d), lambda bi, i: (bi, i, 0))
    col = lambda w: pl.BlockSpec((1, w, tm), lambda bi, i: (bi, 0, i))
    full = lambda a: pl.BlockSpec(a.shape, lambda bi, i: (0,) * a.ndim)
    return pl.pallas_call(
        _merge_kernel,
        grid=(b, s // tm),
        in_specs=[row, col(MLA_WIDTH), col(SB_WIDTH), pl.BlockSpec((1, N_MOD, d), lambda bi, i: (bi, 0, 0)),
                  full(g_mla), full(g_sb), full(wout)],
        out_specs=row,
        out_shape=jax.ShapeDtypeStruct((b, s, d), F32),
        name="merge_out_proj",
    )(x, o_mla, o_sb, mod, g_mla, g_sb, wout)


def _silu(v):
    return v * (1.0 / (1.0 + jnp.exp(-v)))


def _ffn_kernel(x_ref, mod_ref, g_ref, wg_ref, wu_ref, wd_ref, o_ref):
    x = x_ref[0]
    mod = mod_ref[0]
    hb = _modulated_norm(x, g_ref[...], mod[4:5], mod[3:4]).astype(BF16)
    acc = jnp.zeros(x.shape, F32)
    for c in range(wg_ref.shape[1] // FFN_CHUNK):
        lo = c * FFN_CHUNK
        gate = _dot(hb, wg_ref[:, lo:lo + FFN_CHUNK])
        up = _dot(hb, wu_ref[:, lo:lo + FFN_CHUNK])
        acc = acc + _dot((_silu(gate) * up).astype(BF16), wd_ref[lo:lo + FFN_CHUNK, :])
    o_ref[0] = x + mod[5:6] * acc


def _ffn_call(x, mod, g, wg, wu, wd):
    b, s, d = x.shape
    tm = ROW_TILE
    row = pl.BlockSpec((1, tm, d), lambda bi, i: (bi, i, 0))
    full = lambda a: pl.BlockSpec(a.shape, lambda bi, i: (0,) * a.ndim)
    return pl.pallas_call(
        _ffn_kernel,
        grid=(b, s // tm),
        in_specs=[row, pl.BlockSpec((1, N_MOD, d), lambda bi, i: (bi, 0, 0)), full(g), full(wg), full(wu), full(wd)],
        out_specs=row,
        out_shape=jax.ShapeDtypeStruct((b, s, d), F32),
        name="dense_ffn",
    )(x, mod, g, wg, wu, wd)


def _router_kernel(x_ref, mod_ref, g_ref, wr_hi_ref, wr_lo_ref, br_ref, h_ref, info_ref):
    x = x_ref[0]
    mod = mod_ref[0]
    h = _modulated_norm(x, g_ref[...], mod[4:5], mod[3:4])
    h_ref[0] = _pack_bf16_pairs(h)
    h_hi, h_lo = _split_bf16(h)
    w_hi = wr_hi_ref[...]
    logits = _dot(h_hi, w_hi) + _dot(h_lo, w_hi) + _dot(h_hi, wr_lo_ref[...]) + br_ref[...]
    lane = lax.broadcasted_iota(jnp.int32, logits.shape, 1)
    logits = jnp.where(lane < N_EXPERTS, logits, NEG_BIG)
    m1 = jnp.max(logits, axis=-1, keepdims=True)
    i1 = jnp.min(jnp.where(logits == m1, lane, LANES), axis=-1, keepdims=True)
    rest = jnp.where(lane == i1, NEG_BIG, logits)
    m2 = jnp.max(rest, axis=-1, keepdims=True)
    i2 = jnp.min(jnp.where(rest == m2, lane, LANES), axis=-1, keepdims=True)
    e = jnp.exp(m2 - m1)
    w1 = 1.0 / (1.0 + e)
    w2 = e * w1
    info = jnp.where(lane == 0, i1.astype(F32),
                     jnp.where(lane == 1, i2.astype(F32), jnp.where(lane == 2, w1, jnp.where(lane == 3, w2, 0.0))))
    info_ref[0] = info


def _router_call(x, mod, g, wr_hi, wr_lo, br):
    b, s, d = x.shape
    tm = ROW_TILE
    row = pl.BlockSpec((1, tm, d), lambda bi, i: (bi, i, 0))
    full = lambda a: pl.BlockSpec(a.shape, lambda bi, i: (0,) * a.ndim)
    return pl.pallas_call(
        _router_kernel,
        grid=(b, s // tm),
        in_specs=[row, pl.BlockSpec((1, N_MOD, d), lambda bi, i: (bi, 0, 0)), full(g), full(wr_hi), full(wr_lo), full(br)],
        out_specs=[pl.BlockSpec((1, tm, d // 2), lambda bi, i: (bi, i, 0)),
                   pl.BlockSpec((1, tm, LANES), lambda bi, i: (bi, i, 0))],
        out_shape=[jax.ShapeDtypeStruct((b, s, d // 2), jnp.uint32), jax.ShapeDtypeStruct((b, s, LANES), F32)],
        name="moe_router",
    )(x, mod, g, wr_hi, wr_lo, br)


def _row_copy(src_hbm, dst_vmem, sem, src_row, dst_row):
    return pltpu.make_async_copy(src_hbm.at[pl.ds(src_row, 1), :], dst_vmem.at[pl.ds(dst_row, 1), :], sem)


def _start_rows(idx_ref, src_hbm, dst_vmem, sem, n_idx):
    def body(r, carry):
        _row_copy(src_hbm, dst_vmem, sem, idx_ref[jnp.minimum(r, n_idx - 1)], r).start(priority=ROW_COPY_PRIORITY)
        return carry

    lax.fori_loop(0, dst_vmem.shape[0], body, 0, unroll=DMA_UNROLL)


def _wait_rows(src_hbm, dst_vmem, sem):
    def body(r, carry):
        _row_copy(src_hbm, dst_vmem, sem, 0, r).wait()
        return carry

    lax.fori_loop(0, dst_vmem.shape[0], body, 0, unroll=DMA_UNROLL)


def _pack_bf16_pairs(v):
    w = v.shape[1] // 2
    hi = lax.bitcast_convert_type(v[:, :w].astype(BF16).astype(F32), jnp.uint32)
    lo = lax.bitcast_convert_type(v[:, w:].astype(BF16).astype(F32), jnp.uint32)
    return hi | (lo >> 16)


def _unpack_bf16_pairs(p):
    hi = lax.bitcast_convert_type(p & jnp.uint32(0xFFFF0000), F32)
    lo = lax.bitcast_convert_type(p << 16, F32)
    return hi, lo


def _expert_kernel(tile_expert_ref, n_valid_ref, idx_ref, idx_next_ref, h_ref, wg_ref, wu_ref, wd_ref, o_ref,
                   rows_scr, xb_scr, acc_scr, sem, *, n_steps):
    i = pl.program_id(0)
    j = pl.program_id(1)
    n_valid = n_valid_ref[0]
    tm = xb_scr.shape[0]
    per_step = rows_scr.shape[0] // n_steps

    @pl.when((i == 0) & (j == 0))
    def _():
        _start_rows(idx_ref, h_ref, rows_scr, sem, tm)

    @pl.when((i < n_valid) & (j == 0))
    def _():
        _wait_rows(h_ref, rows_scr, sem)
        x_hi, x_lo = _unpack_bf16_pairs(rows_scr[:tm])
        half = x_hi.shape[1]
        xb_scr[:, :half] = x_hi.astype(BF16)
        xb_scr[:, half:] = x_lo.astype(BF16)
        acc_scr[...] = jnp.zeros(acc_scr.shape, F32)

    @pl.when(i < n_valid)
    def _():
        xb = xb_scr[...]
        gate = _dot(xb, wg_ref[0, 0].astype(BF16))
        up = _dot(xb, wu_ref[0, 0].astype(BF16))
        acc_scr[...] += _dot((_silu(gate) * up).astype(BF16), wd_ref[0, 0].astype(BF16))
        for u in range(per_step):
            r = j * per_step + u
            _row_copy(h_ref, rows_scr, sem, idx_next_ref[jnp.minimum(r, tm - 1)], r).start(priority=ROW_COPY_PRIORITY)

    @pl.when((i < n_valid) & (j == n_steps - 1))
    def _():
        o_ref[...] = _pack_bf16_pairs(acc_scr[...])

    @pl.when((i == n_valid - 1) & (j == n_steps - 1))
    def _():
        _wait_rows(h_ref, rows_scr, sem)

    @pl.when((i >= n_valid) & (j == 0))
    def _():
        o_ref[...] = jnp.zeros(o_ref.shape, o_ref.dtype)


def _expert_call(tile_expert, n_valid, row_token, h_flat, wg, wu, wd, layer_idx):
    n_rows = row_token.shape[0]
    d = 2 * h_flat.shape[1]
    tm = EXPERT_ROW_TILE
    tf = EXPERT_FF_TILE
    n_ff = wg.shape[-1]
    n_tiles = n_rows // tm
    n_steps = n_ff // tf
    last_chunk = n_steps - 1
    rows_buf = -(-tm // (n_steps * DMA_UNROLL)) * n_steps * DMA_UNROLL

    def tile(i, nv):
        return jnp.minimum(i, nv[0] - 1)

    def chunk(i, j, nv):
        return jnp.where(i < nv[0], j, last_chunk)

    grid_spec = pltpu.PrefetchScalarGridSpec(
        num_scalar_prefetch=2,
        grid=(n_tiles, n_steps),
        in_specs=[
            pl.BlockSpec((tm,), lambda i, j, te, nv: (tile(i, nv),), memory_space=pltpu.SMEM),
            pl.BlockSpec((tm,), lambda i, j, te, nv: (tile(i + 1, nv),), memory_space=pltpu.SMEM),
            pl.BlockSpec(memory_space=pl.ANY),
            pl.BlockSpec((1, 1, d, tf), lambda i, j, te, nv: (layer_idx, te[tile(i, nv)], 0, chunk(i, j, nv))),
            pl.BlockSpec((1, 1, d, tf), lambda i, j, te, nv: (layer_idx, te[tile(i, nv)], 0, chunk(i, j, nv))),
            pl.BlockSpec((1, 1, tf, d), lambda i, j, te, nv: (layer_idx, te[tile(i, nv)], chunk(i, j, nv), 0)),
        ],
        out_specs=pl.BlockSpec((tm, d // 2), lambda i, j, te, nv: (i, 0)),
        scratch_shapes=[pltpu.VMEM((rows_buf, d // 2), jnp.uint32), pltpu.VMEM((tm, d), BF16), pltpu.VMEM((tm, d), F32),
                        pltpu.SemaphoreType.DMA(())],
    )
    return pl.pallas_call(
        functools.partial(_expert_kernel, n_steps=n_steps),
        grid_spec=grid_spec,
        out_shape=jax.ShapeDtypeStruct((n_rows, d // 2), jnp.uint32),
        name="moe_experts",
    )(tile_expert, n_valid, row_token, row_token, h_flat, wg, wu, wd)


def _combine_kernel(p1_ref, p2_ref, x_ref, info_ref, gate_ref, gfin_ref, y_ref, o_ref, buf1, buf2, sem1, sem2, *, final_norm):
    n = x_ref.shape[0]

    def start(r, _):
        _row_copy(y_ref, buf1, sem1, p1_ref[r], r).start()
        _row_copy(y_ref, buf2, sem2, p2_ref[r], r).start(priority=ROW_COPY_PRIORITY)
        return 0

    def wait(r, _):
        _row_copy(y_ref, buf1, sem1, 0, r).wait()
        _row_copy(y_ref, buf2, sem2, 0, r).wait()
        return 0

    lax.fori_loop(0, n, start, 0, unroll=DMA_UNROLL)
    lax.fori_loop(0, n, wait, 0, unroll=DMA_UNROLL)
    info = info_ref[...]
    a_hi, a_lo = _unpack_bf16_pairs(buf1[...])
    b_hi, b_lo = _unpack_bf16_pairs(buf2[...])
    w1 = info[:, 2:3]
    w2 = info[:, 3:4]
    y = jnp.concatenate([w1 * a_hi + w2 * b_hi, w1 * a_lo + w2 * b_lo], axis=1)
    out = x_ref[...] + gate_ref[0] * y
    if final_norm:
        out = _rms_rows(out) * gfin_ref[...]
    o_ref[...] = out


def _combine_call(p1, p2, x_flat, info_flat, gate_rows, g_final, y_sorted, tokens_per_batch, final_norm):
    t, d = x_flat.shape
    tc = GATHER_TILE
    per_b = tokens_per_batch // tc
    idx = pl.BlockSpec((tc,), lambda i: (i,), memory_space=pltpu.SMEM)
    return pl.pallas_call(
        functools.partial(_combine_kernel, final_norm=final_norm),
        grid=(t // tc,),
        in_specs=[idx, idx, pl.BlockSpec((tc, d), lambda i: (i, 0)), pl.BlockSpec((tc, LANES), lambda i: (i, 0)),
                  pl.BlockSpec((1, 1, d), lambda i: (i // per_b, 0, 0)), pl.BlockSpec((1, d), lambda i: (0, 0)),
                  pl.BlockSpec(memory_space=pl.ANY)],
        out_specs=pl.BlockSpec((tc, d), lambda i: (i, 0)),
        out_shape=jax.ShapeDtypeStruct((t, d), F32),
        scratch_shapes=[pltpu.VMEM((tc, d // 2), jnp.uint32), pltpu.VMEM((tc, d // 2), jnp.uint32),
                        pltpu.SemaphoreType.DMA(()), pltpu.SemaphoreType.DMA(())],
        name="moe_combine",
    )(p1, p2, x_flat, info_flat, gate_rows, g_final, y_sorted)


def _routing_tables(expert_ids, n_tokens):
    tm = EXPERT_ROW_TILE
    n_assign = n_tokens * TOP_K
    n_rows = n_assign + N_EXPERTS * tm
    e_flat = expert_ids.reshape(n_assign)
    onehot = (e_flat[:, None] == jnp.arange(N_EXPERTS, dtype=jnp.int32)[None, :]).astype(jnp.int32)
    rank = jnp.sum((jnp.cumsum(onehot, axis=0) - onehot) * onehot, axis=1)
    counts = jnp.sum(onehot, axis=0)
    padded = ((counts + tm - 1) // tm) * tm
    ends = jnp.cumsum(padded)
    starts = ends - padded
    pos = (starts[e_flat] + rank).astype(jnp.int32)
    row_token = jnp.zeros((n_rows,), jnp.int32).at[pos].set(jnp.arange(n_assign, dtype=jnp.int32) // TOP_K)
    n_valid = (ends[-1] // tm).astype(jnp.int32).reshape(1)
    tile_start = jnp.arange(n_rows // tm, dtype=jnp.int32) * tm
    tile_expert = jnp.minimum(jnp.sum((ends[None, :] <= tile_start[:, None]).astype(jnp.int32), axis=1), N_EXPERTS - 1)
    pos2 = pos.reshape(n_tokens, TOP_K)
    return row_token, tile_expert, n_valid, pos2[:, 0], pos2[:, 1]


def _prep_w_in(w_in):
    d = w_in.shape[0]
    cq, ckv, kr, qsb, ksb, vsb = jnp.split(
        w_in, (Q_LORA, Q_LORA + KV_LORA, Q_LORA + KV_LORA + MLA_ROPE,
               Q_LORA + KV_LORA + MLA_ROPE + SB_WIDTH, Q_LORA + KV_LORA + MLA_ROPE + 2 * SB_WIDTH), axis=1)
    z = lambda n: jnp.zeros((d, n), w_in.dtype)
    kr_blk = jnp.concatenate([z(MLA_NOPE), kr, z(LANES - MLA_QK_DIM)], axis=1)
    kr_swap = jnp.concatenate([z(MLA_NOPE), kr[:, HALF_ROPE:], kr[:, :HALF_ROPE], z(LANES - MLA_QK_DIM)], axis=1)
    return jnp.concatenate([cq, ckv, kr_blk, kr_swap, qsb * SB_HEAD_DIM ** -0.5, ksb, vsb], axis=1).astype(BF16)


def _prep_w_uq(w_uq):
    r = w_uq.shape[0]
    w = w_uq.reshape(r, MLA_HEADS, MLA_QK_DIM)
    nope, x1, x2 = w[..., :MLA_NOPE], w[..., MLA_NOPE:MLA_NOPE + HALF_ROPE], w[..., MLA_NOPE + HALF_ROPE:]
    pad = jnp.zeros((r, MLA_HEADS, LANES - MLA_QK_DIM), w.dtype)
    plain = jnp.concatenate([nope, x1, x2, pad], axis=-1).reshape(r, MLA_PAD_WIDTH)
    swapped = jnp.concatenate([jnp.zeros_like(nope), x2, x1, pad], axis=-1).reshape(r, MLA_PAD_WIDTH)
    return jnp.concatenate([plain, swapped], axis=1).astype(BF16)


def _prep_w_ukv(w_ukv):
    r = w_ukv.shape[0]
    w = w_ukv.reshape(r, MLA_HEADS, MLA_NOPE + MLA_V)
    k_nope, v = w[..., :MLA_NOPE], w[..., MLA_NOPE:]
    k_pad = jnp.concatenate([k_nope, jnp.zeros((r, MLA_HEADS, LANES - MLA_NOPE), w.dtype)], axis=-1)
    return jnp.concatenate([k_pad.reshape(r, MLA_PAD_WIDTH), v.reshape(r, MLA_WIDTH)], axis=1).astype(BF16)


def kernel(x, c, positions, w_ada, b_ada, g_mix, g_ffn, w_in, g_q_lat, w_uq, g_kv_lat, w_ukv, g_grp_mla, g_grp_sb, w_out, w_ff_gate, w_ff_up, w_ff_down, w_router, b_router, w_ex_gate, w_ex_up, w_ex_down, g_final):
    b, s, d = x.shape
    depth = w_ada.shape[0]
    n_tokens = b * s
    assert depth % 2 == 0, "the final norm is fused into the routed mixer of the last layer"

    c_pad = jnp.zeros((8, d), F32).at[:b].set(c)
    mod_all = _ada_call(c_pad, w_ada, b_ada)[:, :b].reshape(depth, b, N_MOD, d)

    lane = jnp.arange(LANES)
    freq = ROPE_BASE ** (-((lane - MLA_NOPE) % HALF_ROPE).astype(F32) / HALF_ROPE)
    pos_lanes = jnp.broadcast_to(positions.astype(F32)[..., None], (b, s, LANES))
    cos_t, sin_t = _rope_tables(pos_lanes, freq.reshape(1, LANES))

    row2 = lambda v: v.reshape(1, -1)
    for layer in range(depth):
        mod = mod_all[layer]
        qm, km, vtm, qs, ks, vts = _proj_call(
            x, mod, row2(g_mix[layer]), cos_t, sin_t, _prep_w_in(w_in[layer]), row2(g_q_lat[layer]),
            _prep_w_uq(w_uq[layer]), row2(g_kv_lat[layer]), _prep_w_ukv(w_ukv[layer]))
        o_mla = _mla_call(qm, km, vtm)
        o_sb = _sb_call(qs, ks, vts)
        x = _merge_call(x, o_mla, o_sb, mod, row2(g_grp_mla[layer]), row2(g_grp_sb[layer]), w_out[layer].astype(BF16))

        i = layer // 2
        if layer % 2 == 0:
            x = _ffn_call(x, mod, row2(g_ffn[layer]), w_ff_gate[i].astype(BF16), w_ff_up[i].astype(BF16),
                          w_ff_down[i].astype(BF16))
        else:
            wr = jnp.zeros((d, LANES), F32).at[:, :N_EXPERTS].set(w_router[i])
            wr_hi = wr.astype(BF16)
            wr_lo = (wr - wr_hi.astype(F32)).astype(BF16)
            br = jnp.zeros((1, LANES), F32).at[0, :N_EXPERTS].set(b_router[i])
            h, info = _router_call(x, mod, row2(g_ffn[layer]), wr_hi, wr_lo, br)
            info_flat = info.reshape(n_tokens, LANES)
            expert_ids = info_flat[:, :TOP_K].astype(jnp.int32)
            row_token, tile_expert, n_valid, p1, p2 = _routing_tables(expert_ids, n_tokens)
            ys = _expert_call(tile_expert, n_valid, row_token, h.reshape(n_tokens, d // 2), w_ex_gate, w_ex_up, w_ex_down, i)
            final = layer == depth - 1
            x = _combine_call(p1, p2, x.reshape(n_tokens, d), info_flat, mod[:, 5:6, :], row2(g_final), ys, s,
                              final_norm=final).reshape(b, s, d)
    return x
```

```python
import functools

import jax
import jax.numpy as jnp
from jax import lax
from jax.experimental import pallas as pl
from jax.experimental.pallas import tpu as pltpu

F32 = jnp.float32
BF16 = jnp.bfloat16

EPS = 1e-6
N_MOD = 6
CHUNK = 64
MLA_HEADS = 8
MLA_NOPE = 64
MLA_ROPE = 32
MLA_V = 64
Q_LORA = 256
KV_LORA = 128
ROPE_BASE = 10000.0
SB_HEADS = 8
SB_HEAD_DIM = 64
N_EXPERTS = 8
TOP_K = 2

LANES = 128
HALF_ROPE = MLA_ROPE // 2
MLA_QK_DIM = MLA_NOPE + MLA_ROPE
LOG2E = 1.4426950408889634
NEG_BIG = -1e30

ROW_TILE = 512
ATT_TILE = 256
EXPERT_ROW_TILE = 1024
EXPERT_FF_TILE = 512
GATHER_TILE = 1024
FFN_CHUNK = 512
DMA_UNROLL = 8


def _split_bf16(v):
    hi = v.astype(BF16)
    lo = (v - hi.astype(F32)).astype(BF16)
    return hi, lo


def _dot(a, b):
    return jnp.dot(a, b, preferred_element_type=F32)


def _rms_rows(v):
    return v * lax.rsqrt(jnp.mean(v * v, axis=-1, keepdims=True) + EPS)


def _modulated_norm(x, gain, scale, shift):
    return _rms_rows(x) * (gain * (1.0 + scale)) + shift


def _ada_kernel(c_ref, w_ref, b_ref, o_ref):
    c = c_ref[...]
    act = c * (1.0 / (1.0 + jnp.exp(-c)))
    a_hi, a_lo = _split_bf16(act)
    w_hi, w_lo = _split_bf16(w_ref[0])
    o_ref[0] = _dot(a_hi, w_hi) + _dot(a_lo, w_hi) + _dot(a_hi, w_lo) + b_ref[0]


def _ada_call(c_pad, w_ada, b_ada):
    depth, d, n = w_ada.shape
    tn = n // 4
    return pl.pallas_call(
        _ada_kernel,
        grid=(depth, n // tn),
        in_specs=[
            pl.BlockSpec((8, d), lambda l, j: (0, 0)),
            pl.BlockSpec((1, d, tn), lambda l, j: (l, 0, j)),
            pl.BlockSpec((1, 1, tn), lambda l, j: (l, 0, j)),
        ],
        out_specs=pl.BlockSpec((1, 8, tn), lambda l, j: (l, 0, j)),
        out_shape=jax.ShapeDtypeStruct((depth, 8, n), F32),
        name="ada_mod",
    )(c_pad, w_ada, b_ada.reshape(depth, 1, n))


def _rope_table_kernel(pos_ref, freq_ref, cos_ref, sin_ref):
    ang = pos_ref[0] * freq_ref[...]
    lane = lax.broadcasted_iota(jnp.int32, ang.shape, 1)
    in_rope = (lane >= MLA_NOPE) & (lane < MLA_QK_DIM)
    first_half = lane < MLA_NOPE + HALF_ROPE
    cos_ref[0] = jnp.where(in_rope, jnp.cos(ang), jnp.where(lane < MLA_NOPE, 1.0, 0.0))
    s = jnp.sin(ang)
    sin_ref[0] = jnp.where(in_rope, jnp.where(first_half, -s, s), 0.0)


def _rope_tables(pos_lanes, freq_lanes):
    b, s, _ = pos_lanes.shape
    tm = ROW_TILE
    spec = pl.BlockSpec((1, tm, LANES), lambda bi, i: (bi, i, 0))
    return pl.pallas_call(
        _rope_table_kernel,
        grid=(b, s // tm),
        in_specs=[spec, pl.BlockSpec((1, LANES), lambda bi, i: (0, 0))],
        out_specs=[spec, spec],
        out_shape=[jax.ShapeDtypeStruct((b, s, LANES), F32)] * 2,
        name="rope_tables",
    )(pos_lanes, freq_lanes)


IN_OFF_CQ = 0
IN_OFF_CKV = Q_LORA
IN_OFF_KR = IN_OFF_CKV + KV_LORA
IN_OFF_KRS = IN_OFF_KR + LANES
IN_OFF_QSB = IN_OFF_KRS + LANES
SB_WIDTH = SB_HEADS * SB_HEAD_DIM
MLA_WIDTH = MLA_HEADS * MLA_V
IN_OFF_KSB = IN_OFF_QSB + SB_WIDTH
IN_OFF_VSB = IN_OFF_KSB + SB_WIDTH
IN_COLS_PADDED = IN_OFF_VSB + SB_WIDTH
MLA_PAD_WIDTH = MLA_HEADS * LANES
MLA_V_EXT = MLA_V + 16


def _proj_kernel(x_ref, mod_ref, gmix_ref, cos_ref, sin_ref, win_ref, gq_ref, wuq_ref, gkv_ref, wukv_ref,
                 qm_ref, km_ref, vtm_ref, qs_ref, ks_ref, vts_ref):
    x = x_ref[0]
    mod = mod_ref[0]
    h = _modulated_norm(x, gmix_ref[...], mod[1:2], mod[0:1])
    proj = _dot(h.astype(BF16), win_ref[...])
    cos = cos_ref[0]
    sin = sin_ref[0]
    n_kv = vtm_ref.shape[1]
    tk = vtm_ref.shape[3]

    cqn = _rms_rows(proj[:, IN_OFF_CQ:IN_OFF_CQ + Q_LORA]) * gq_ref[...]
    q2 = _dot(cqn.astype(BF16), wuq_ref[...])
    q_scale = MLA_QK_DIM ** -0.5 * LOG2E
    for hd in range(MLA_HEADS):
        lo = hd * LANES
        q = q2[:, lo:lo + LANES] * cos + q2[:, MLA_PAD_WIDTH + lo:MLA_PAD_WIDTH + lo + LANES] * sin
        qm_ref[0, :, lo:lo + LANES] = (q * q_scale).astype(qm_ref.dtype)

    ckvn = _rms_rows(proj[:, IN_OFF_CKV:IN_OFF_CKV + KV_LORA]) * gkv_ref[...]
    kv2 = _dot(ckvn.astype(BF16), wukv_ref[...])
    k_rope = proj[:, IN_OFF_KR:IN_OFF_KR + LANES] * cos + proj[:, IN_OFF_KRS:IN_OFF_KRS + LANES] * sin
    for hd in range(MLA_HEADS):
        lo = hd * LANES
        km_ref[0, :, lo:lo + LANES] = (kv2[:, lo:lo + LANES] + k_rope).astype(km_ref.dtype)
    v_t = kv2[:, MLA_PAD_WIDTH:].T
    pad_rows = MLA_V_EXT - MLA_V
    ones_row = jnp.where(lax.broadcasted_iota(jnp.int32, (pad_rows, tk), 0) == 0, 1.0, 0.0).astype(vtm_ref.dtype)
    for c in range(n_kv):
        for hd in range(MLA_HEADS):
            lo = hd * MLA_V_EXT
            vtm_ref[0, c, lo:lo + MLA_V, :] = v_t[hd * MLA_V:(hd + 1) * MLA_V, c * tk:(c + 1) * tk].astype(vtm_ref.dtype)
            vtm_ref[0, c, lo + MLA_V:lo + MLA_V_EXT, :] = ones_row

    qs_ref[0] = proj[:, IN_OFF_QSB:IN_OFF_QSB + SB_WIDTH].astype(qs_ref.dtype)
    ks_ref[0] = proj[:, IN_OFF_KSB:IN_OFF_KSB + SB_WIDTH].astype(ks_ref.dtype)
    vs_t = proj[:, IN_OFF_VSB:IN_OFF_VSB + SB_WIDTH].T
    for c in range(n_kv):
        vts_ref[0, c] = vs_t[:, c * tk:(c + 1) * tk].astype(vts_ref.dtype)


def _proj_call(x, mod, gmix, cos_t, sin_t, win, gq, wuq, gkv, wukv):
    b, s, d = x.shape
    tm = ROW_TILE
    tk = ATT_TILE
    n_kv = tm // tk
    row = lambda w: pl.BlockSpec((1, tm, w), lambda bi, i: (bi, i, 0))
    full = lambda a: pl.BlockSpec(a.shape, lambda bi, i: (0,) * a.ndim)
    vt_spec = lambda w: pl.BlockSpec((1, n_kv, w, tk), lambda bi, i: (bi, i, 0, 0))
    return pl.pallas_call(
        _proj_kernel,
        grid=(b, s // tm),
        in_specs=[row(d), pl.BlockSpec((1, N_MOD, d), lambda bi, i: (bi, 0, 0)), full(gmix), row(LANES), row(LANES),
                  full(win), full(gq), full(wuq), full(gkv), full(wukv)],
        out_specs=[row(MLA_PAD_WIDTH), row(MLA_PAD_WIDTH), vt_spec(MLA_HEADS * MLA_V_EXT),
                   row(SB_WIDTH), row(SB_WIDTH), vt_spec(SB_WIDTH)],
        out_shape=[
            jax.ShapeDtypeStruct((b, s, MLA_PAD_WIDTH), BF16),
            jax.ShapeDtypeStruct((b, s, MLA_PAD_WIDTH), BF16),
            jax.ShapeDtypeStruct((b, s // tk, MLA_HEADS * MLA_V_EXT, tk), BF16),
            jax.ShapeDtypeStruct((b, s, SB_WIDTH), BF16),
            jax.ShapeDtypeStruct((b, s, SB_WIDTH), BF16),
            jax.ShapeDtypeStruct((b, s // tk, SB_WIDTH, tk), BF16),
        ],
        name="in_proj",
    )(x, mod, gmix, cos_t, sin_t, win, gq, wuq, gkv, wukv)


def _resident(shape, index_map):
    return pl.BlockSpec(shape, index_map, pipeline_mode=pl.Buffered(1))


def _mla_kernel(q_ref, k_ref, vt_ref, o_ref, qt_scr, sc_a, sc_b, m_scr, acc_scr):
    tq = q_ref.shape[1]
    tk = vt_ref.shape[3]
    i = pl.program_id(1)
    for hd in range(MLA_HEADS):
        qt_scr[hd] = q_ref[0, :, hd * LANES:(hd + 1) * LANES].astype(F32).T.astype(BF16)
    m_scr[...] = jnp.full(m_scr.shape, NEG_BIG, F32)
    acc_scr[...] = jnp.zeros(acc_scr.shape, F32)

    def scores_into(j, sc):
        start = pl.multiple_of(j * tk, tk)
        for hd in range(MLA_HEADS):
            sc[hd] = _dot(k_ref[0, pl.ds(start, tk), hd * LANES:(hd + 1) * LANES], qt_scr[hd])

    def softmax_pv(j, sc, masked):
        probs = []
        alphas = []
        for hd in range(MLA_HEADS):
            s = sc[hd]
            if masked:
                key_chunk = lax.broadcasted_iota(jnp.int32, s.shape, 0) // CHUNK
                qry_chunk = lax.broadcasted_iota(jnp.int32, s.shape, 1) // CHUNK
                s = jnp.where(key_chunk <= qry_chunk, s, NEG_BIG)
            m = m_scr[hd]
            m_new = jnp.maximum(m, jnp.max(s, axis=0, keepdims=True))
            m_scr[hd] = m_new
            alphas.append(jnp.exp2(m - m_new))
            probs.append(jnp.exp2(s - m_new).astype(BF16))
        for hd in range(MLA_HEADS):
            pv = _dot(vt_ref[0, j, hd * MLA_V_EXT:(hd + 1) * MLA_V_EXT, :], probs[hd])
            acc_scr[hd] = alphas[hd] * acc_scr[hd] + pv

    def two_blocks(j):
        scores_into(j + 1, sc_b)
        softmax_pv(j, sc_a, masked=False)
        scores_into(j + 2, sc_a)
        softmax_pv(j + 1, sc_b, masked=False)

    def four_blocks(t, carry):
        two_blocks(4 * t)
        two_blocks(4 * t + 2)
        return carry

    scores_into(0, sc_a)
    lax.fori_loop(0, i // 4, four_blocks, 0)
    rem = i % 4

    @pl.when(rem >= 2)
    def _():
        two_blocks(i - rem)

    @pl.when(rem % 2 == 1)
    def _():
        scores_into(i, sc_b)
        softmax_pv(i - 1, sc_a, masked=False)
        softmax_pv(i, sc_b, masked=True)

    @pl.when(rem % 2 == 0)
    def _():
        softmax_pv(i, sc_a, masked=True)

    for hd in range(MLA_HEADS):
        acc = acc_scr[hd]
        o_ref[0, hd * MLA_V:(hd + 1) * MLA_V, :] = acc[:MLA_V] * (1.0 / acc[MLA_V:MLA_V + 1])


def _mla_call(q, k, vt):
    b, s, _ = q.shape
    n_kv, tk = vt.shape[1], vt.shape[3]
    tq = tk
    scores = pltpu.VMEM((MLA_HEADS, tk, tq), F32)
    return pl.pallas_call(
        _mla_kernel,
        grid=(b, s // tq),
        in_specs=[
            pl.BlockSpec((1, tq, MLA_PAD_WIDTH), lambda bi, i: (bi, i, 0)),
            _resident((1, s, MLA_PAD_WIDTH), lambda bi, i: (bi, 0, 0)),
            _resident((1, n_kv, MLA_HEADS * MLA_V_EXT, tk), lambda bi, i: (bi, 0, 0, 0)),
        ],
        out_specs=pl.BlockSpec((1, MLA_WIDTH, tq), lambda bi, i: (bi, 0, i)),
        out_shape=jax.ShapeDtypeStruct((b, MLA_WIDTH, s), F32),
        scratch_shapes=[pltpu.VMEM((MLA_HEADS, LANES, tq), BF16), scores, scores,
                        pltpu.VMEM((MLA_HEADS, 1, tq), F32), pltpu.VMEM((MLA_HEADS, MLA_V_EXT, tq), F32)],
        name="mla_attention",
    )(q, k, vt)


F32_EXP2_ZERO_BELOW = -152.0


def _sb_kernel(q_ref, k_ref, vt_ref, o_ref, qt_scr, later_scr, run_scr, acc_scr):
    tq = q_ref.shape[1]
    tk = vt_ref.shape[3]
    i = pl.program_id(1)
    for pair in range(SB_HEADS // 2):
        q_t = q_ref[0, :, pair * LANES:(pair + 1) * LANES].astype(F32).T * LOG2E
        head_row = lax.broadcasted_iota(jnp.int32, q_t.shape, 0) // SB_HEAD_DIM
        for sub in range(2):
            qt_scr[2 * pair + sub] = jnp.where(head_row == sub, q_t, 0.0).astype(BF16)
    r = lax.broadcasted_iota(jnp.int32, later_scr.shape, 0)
    c = lax.broadcasted_iota(jnp.int32, later_scr.shape, 1) % tk
    later_scr[...] = jnp.where(((r < tk) & (c > r)) | (r == tk), 1.0, 0.0).astype(BF16)
    run_scr[...] = jnp.zeros(run_scr.shape, F32)
    acc_scr[...] = jnp.zeros(acc_scr.shape, F32)

    def step(j, masked):
        start = pl.multiple_of(j * tk, tk)
        zs = [_dot(k_ref[0, pl.ds(start, tk), (hd // 2) * LANES:(hd // 2 + 1) * LANES], qt_scr[hd])
              for hd in range(SB_HEADS)]
        if masked:
            strict = lax.broadcasted_iota(jnp.int32, (tk, tq), 0) < lax.broadcasted_iota(jnp.int32, (tk, tq), 1)
        log_betas, laters = [], []
        for hd in range(SB_HEADS):
            z = zs[hd]
            soft = jnp.log2(1.0 + jnp.exp2(-jnp.abs(z)))
            drop = jnp.maximum(z, 0.0) + soft
            log_betas.append(jnp.minimum(z, 0.0) - soft)
            if masked:
                drop = jnp.where(strict, drop, 0.0)
            drop_hi, drop_lo = _split_bf16(drop)
            laters.append(_dot(later_scr[...], jnp.concatenate([drop_hi, drop_lo], axis=0)))
        weights = []
        for hd in range(SB_HEADS):
            run = run_scr[hd]
            a = jnp.exp2(log_betas[hd] - laters[hd][:tk] - run)
            if masked:
                a = jnp.where(strict, a, 0.0)
            weights.append(a.astype(BF16))
            run_scr[hd] = run + laters[hd][tk:tk + 1]
        for hd in range(SB_HEADS):
            acc_scr[hd] += _dot(vt_ref[0, j, hd * SB_HEAD_DIM:(hd + 1) * SB_HEAD_DIM, :], weights[hd])

    def any_weight_left():
        return (jnp.min(run_scr[...]) < -F32_EXP2_ZERO_BELOW).astype(jnp.int32)

    step(i, masked=True)

    def cond(carry):
        j, alive = carry
        return (j >= 0) & (alive > 0)

    def body(carry):
        j, _ = carry
        step(j, masked=False)
        return j - 1, any_weight_left()

    lax.while_loop(cond, body, (i - 1, any_weight_left()))
    for hd in range(SB_HEADS):
        o_ref[0, hd * SB_HEAD_DIM:(hd + 1) * SB_HEAD_DIM, :] = acc_scr[hd]


def _sb_call(q, k, vt):
    b, s, _ = q.shape
    n_kv, tk = vt.shape[1], vt.shape[3]
    tq = tk
    return pl.pallas_call(
        _sb_kernel,
        grid=(b, s // tq),
        in_specs=[
            pl.BlockSpec((1, tq, SB_WIDTH), lambda bi, i: (bi, i, 0)),
            _resident((1, s, SB_WIDTH), lambda bi, i: (bi, 0, 0)),
            _resident((1, n_kv, SB_WIDTH, tk), lambda bi, i: (bi, 0, 0, 0)),
        ],
        out_specs=pl.BlockSpec((1, SB_WIDTH, tq), lambda bi, i: (bi, 0, i)),
        out_shape=jax.ShapeDtypeStruct((b, SB_WIDTH, s), F32),
        scratch_shapes=[pltpu.VMEM((SB_HEADS, LANES, tq), BF16), pltpu.VMEM((tk + 16, 2 * tk), BF16),
                        pltpu.VMEM((SB_HEADS, 1, tq), F32), pltpu.VMEM((SB_HEADS, SB_HEAD_DIM, tq), F32)],
        name="sb_attention",
    )(q, k, vt)


def _merge_kernel(x_ref, om_ref, os_ref, mod_ref, gm_ref, gs_ref, wout_ref, o_ref):
    def group(o_t, gain):
        inv = lax.rsqrt(jnp.mean(o_t * o_t, axis=0, keepdims=True) + EPS)
        return ((o_t * inv).T * gain).astype(BF16)

    a = group(om_ref[0], gm_ref[...])
    bb = group(os_ref[0], gs_ref[...])
    mixed = _dot(a, wout_ref[:MLA_WIDTH, :]) + _dot(bb, wout_ref[MLA_WIDTH:, :])
    o_ref[0] = x_ref[0] + mod_ref[0][2:3] * mixed


def _merge_call(x, o_mla, o_sb, mod, g_mla, g_sb, wout):
    b, s, d = x.shape
    tm = ROW_TILE
    row = pl.BlockSpec((1, tm, d), lambda bi, i: (bi, i, 0))
    col = lambda w: pl.BlockSpec((1, w, tm), lambda bi, i: (bi, 0, i))
    full = lambda a: pl.BlockSpec(a.shape, lambda bi, i: (0,) * a.ndim)
    return pl.pallas_call(
        _merge_kernel,
        grid=(b, s // tm),
        in_specs=[row, col(MLA_WIDTH), col(SB_WIDTH), pl.BlockSpec((1, N_MOD, d), lambda bi, i: (bi, 0, 0)),
                  full(g_mla), full(g_sb), full(wout)],
        out_specs=row,
        out_shape=jax.ShapeDtypeStruct((b, s, d), F32),
        name="merge_out_proj",
    )(x, o_mla, o_sb, mod, g_mla, g_sb, wout)


def _silu(v):
    return v * (1.0 / (1.0 + jnp.exp(-v)))


def _ffn_kernel(x_ref, mod_ref, g_ref, wg_ref, wu_ref, wd_ref, o_ref):
    x = x_ref[0]
    mod = mod_ref[0]
    hb = _modulated_norm(x, g_ref[...], mod[4:5], mod[3:4]).astype(BF16)
    acc = jnp.zeros(x.shape, F32)
    for c in range(wg_ref.shape[1] // FFN_CHUNK):
        lo = c * FFN_CHUNK
        gate = _dot(hb, wg_ref[:, lo:lo + FFN_CHUNK])
        up = _dot(hb, wu_ref[:, lo:lo + FFN_CHUNK])
        acc = acc + _dot((_silu(gate) * up).astype(BF16), wd_ref[lo:lo + FFN_CHUNK, :])
    o_ref[0] = x + mod[5:6] * acc


def _ffn_call(x, mod, g, wg, wu, wd):
    b, s, d = x.shape
    tm = ROW_TILE
    row = pl.BlockSpec((1, tm, d), lambda bi, i: (bi, i, 0))
    full = lambda a: pl.BlockSpec(a.shape, lambda bi, i: (0,) * a.ndim)
    return pl.pallas_call(
        _ffn_kernel,
        grid=(b, s // tm),
        in_specs=[row, pl.BlockSpec((1, N_MOD, d), lambda bi, i: (bi, 0, 0)), full(g), full(wg), full(wu), full(wd)],
        out_specs=row,
        out_shape=jax.ShapeDtypeStruct((b, s, d), F32),
        name="dense_ffn",
    )(x, mod, g, wg, wu, wd)


def _router_kernel(x_ref, mod_ref, g_ref, wr_hi_ref, wr_lo_ref, br_ref, h_ref, info_ref):
    x = x_ref[0]
    mod = mod_ref[0]
    h = _modulated_norm(x, g_ref[...], mod[4:5], mod[3:4])
    h_ref[0] = _pack_bf16_pairs(h)
    h_hi, h_lo = _split_bf16(h)
    w_hi = wr_hi_ref[...]
    logits = _dot(h_hi, w_hi) + _dot(h_lo, w_hi) + _dot(h_hi, wr_lo_ref[...]) + br_ref[...]
    lane = lax.broadcasted_iota(jnp.int32, logits.shape, 1)
    logits = jnp.where(lane < N_EXPERTS, logits, NEG_BIG)
    m1 = jnp.max(logits, axis=-1, keepdims=True)
    i1 = jnp.min(jnp.where(logits == m1, lane, LANES), axis=-1, keepdims=True)
    rest = jnp.where(lane == i1, NEG_BIG, logits)
    m2 = jnp.max(rest, axis=-1, keepdims=True)
    i2 = jnp.min(jnp.where(rest == m2, lane, LANES), axis=-1, keepdims=True)
    e = jnp.exp(m2 - m1)
    w1 = 1.0 / (1.0 + e)
    w2 = e * w1
    info = jnp.where(lane == 0, i1.astype(F32),
                     jnp.where(lane == 1, i2.astype(F32), jnp.where(lane == 2, w1, jnp.where(lane == 3, w2, 0.0))))
    info_ref[0] = info


def _router_call(x, mod, g, wr_hi, wr_lo, br):
    b, s, d = x.shape
    tm = ROW_TILE
    row = pl.BlockSpec((1, tm, d), lambda bi, i: (bi, i, 0))
    full = lambda a: pl.BlockSpec(a.shape, lambda bi, i: (0,) * a.ndim)
    return pl.pallas_call(
        _router_kernel,
        grid=(b, s // tm),
        in_specs=[row, pl.BlockSpec((1, N_MOD, d), lambda bi, i: (bi, 0, 0)), full(g), full(wr_hi), full(wr_lo), full(br)],
        out_specs=[pl.BlockSpec((1, tm, d // 2), lambda bi, i: (bi, i, 0)),
                   pl.BlockSpec((1, tm, LANES), lambda bi, i: (bi, i, 0))],
        out_shape=[jax.ShapeDtypeStruct((b, s, d // 2), jnp.uint32), jax.ShapeDtypeStruct((b, s, LANES), F32)],
        name="moe_router",
    )(x, mod, g, wr_hi, wr_lo, br)


def _row_copy(src_hbm, dst_vmem, sem, src_row, dst_row):
    return pltpu.make_async_copy(src_hbm.at[pl.ds(src_row, 1), :], dst_vmem.at[pl.ds(dst_row, 1), :], sem)


def _start_rows(idx_ref, src_hbm, dst_vmem, sem, n_idx):
    def body(r, carry):
        _row_copy(src_hbm, dst_vmem, sem, idx_ref[jnp.minimum(r, n_idx - 1)], r).start()
        return carry

    lax.fori_loop(0, dst_vmem.shape[0], body, 0, unroll=DMA_UNROLL)


def _wait_rows(src_hbm, dst_vmem, sem):
    def body(r, carry):
        _row_copy(src_hbm, dst_vmem, sem, 0, r).wait()
        return carry

    lax.fori_loop(0, dst_vmem.shape[0], body, 0, unroll=DMA_UNROLL)


def _pack_bf16_pairs(v):
    w = v.shape[1] // 2
    hi = lax.bitcast_convert_type(v[:, :w].astype(BF16).astype(F32), jnp.uint32)
    lo = lax.bitcast_convert_type(v[:, w:].astype(BF16).astype(F32), jnp.uint32)
    return hi | (lo >> 16)


def _unpack_bf16_pairs(p):
    hi = lax.bitcast_convert_type(p & jnp.uint32(0xFFFF0000), F32)
    lo = lax.bitcast_convert_type(p << 16, F32)
    return hi, lo


def _gather_kernel(idx_ref, src_ref, o_ref, sem):
    _start_rows(idx_ref, src_ref, o_ref, sem, o_ref.shape[0])
    _wait_rows(src_ref, o_ref, sem)


def _gather_call(row_token, h_packed):
    n_rows = row_token.shape[0]
    w = h_packed.shape[1]
    tg = GATHER_TILE
    return pl.pallas_call(
        _gather_kernel,
        grid=(n_rows // tg,),
        in_specs=[pl.BlockSpec((tg,), lambda i: (i,), memory_space=pltpu.SMEM), pl.BlockSpec(memory_space=pl.ANY)],
        out_specs=pl.BlockSpec((tg, w), lambda i: (i, 0)),
        out_shape=jax.ShapeDtypeStruct((n_rows, w), h_packed.dtype),
        scratch_shapes=[pltpu.SemaphoreType.DMA(())],
        name="moe_gather",
    )(row_token, h_packed)


def _expert_kernel(tile_expert_ref, n_valid_ref, xs_ref, wg_ref, wu_ref, wd_ref, o_ref, xb_scr, acc_scr, *, n_steps):
    i = pl.program_id(0)
    j = pl.program_id(1)
    n_valid = n_valid_ref[0]

    @pl.when((i < n_valid) & (j == 0))
    def _():
        x_hi, x_lo = _unpack_bf16_pairs(xs_ref[...])
        half = x_hi.shape[1]
        xb_scr[:, :half] = x_hi.astype(BF16)
        xb_scr[:, half:] = x_lo.astype(BF16)
        acc_scr[...] = jnp.zeros(acc_scr.shape, F32)

    @pl.when(i < n_valid)
    def _():
        xb = xb_scr[...]
        gate = _dot(xb, wg_ref[0, 0].astype(BF16))
        up = _dot(xb, wu_ref[0, 0].astype(BF16))
        acc_scr[...] += _dot((_silu(gate) * up).astype(BF16), wd_ref[0, 0].astype(BF16))

    @pl.when((i < n_valid) & (j == n_steps - 1))
    def _():
        o_ref[...] = _pack_bf16_pairs(acc_scr[...])

    @pl.when((i >= n_valid) & (j == 0))
    def _():
        o_ref[...] = jnp.zeros(o_ref.shape, o_ref.dtype)


def _expert_call(tile_expert, n_valid, xs, wg, wu, wd, layer_idx):
    n_rows = xs.shape[0]
    d = 2 * xs.shape[1]
    tm = EXPERT_ROW_TILE
    tf = EXPERT_FF_TILE
    n_ff = wg.shape[-1]
    n_tiles = n_rows // tm
    n_steps = n_ff // tf
    last_chunk = n_steps - 1

    def tile(i, nv):
        return jnp.minimum(i, nv[0] - 1)

    def chunk(i, j, nv):
        return jnp.where(i < nv[0], j, last_chunk)

    grid_spec = pltpu.PrefetchScalarGridSpec(
        num_scalar_prefetch=2,
        grid=(n_tiles, n_steps),
        in_specs=[
            pl.BlockSpec((tm, d // 2), lambda i, j, te, nv: (tile(i, nv), 0)),
            pl.BlockSpec((1, 1, d, tf), lambda i, j, te, nv: (layer_idx, te[tile(i, nv)], 0, chunk(i, j, nv))),
            pl.BlockSpec((1, 1, d, tf), lambda i, j, te, nv: (layer_idx, te[tile(i, nv)], 0, chunk(i, j, nv))),
            pl.BlockSpec((1, 1, tf, d), lambda i, j, te, nv: (layer_idx, te[tile(i, nv)], chunk(i, j, nv), 0)),
        ],
        out_specs=pl.BlockSpec((tm, d // 2), lambda i, j, te, nv: (i, 0)),
        scratch_shapes=[pltpu.VMEM((tm, d), BF16), pltpu.VMEM((tm, d), F32)],
    )
    return pl.pallas_call(
        functools.partial(_expert_kernel, n_steps=n_steps),
        grid_spec=grid_spec,
        out_shape=jax.ShapeDtypeStruct((n_rows, d // 2), jnp.uint32),
        name="moe_experts",
    )(tile_expert, n_valid, xs, wg, wu, wd)


def _combine_kernel(p1_ref, p2_ref, x_ref, info_ref, gate_ref, gfin_ref, y_ref, o_ref, buf1, buf2, sem1, sem2, *, final_norm):
    n = x_ref.shape[0]

    def start(r, _):
        _row_copy(y_ref, buf1, sem1, p1_ref[r], r).start()
        _row_copy(y_ref, buf2, sem2, p2_ref[r], r).start()
        return 0

    def wait(r, _):
        _row_copy(y_ref, buf1, sem1, 0, r).wait()
        _row_copy(y_ref, buf2, sem2, 0, r).wait()
        return 0

    lax.fori_loop(0, n, start, 0, unroll=DMA_UNROLL)
    lax.fori_loop(0, n, wait, 0, unroll=DMA_UNROLL)
    info = info_ref[...]
    a_hi, a_lo = _unpack_bf16_pairs(buf1[...])
    b_hi, b_lo = _unpack_bf16_pairs(buf2[...])
    w1 = info[:, 2:3]
    w2 = info[:, 3:4]
    y = jnp.concatenate([w1 * a_hi + w2 * b_hi, w1 * a_lo + w2 * b_lo], axis=1)
    out = x_ref[...] + gate_ref[0] * y
    if final_norm:
        out = _rms_rows(out) * gfin_ref[...]
    o_ref[...] = out


def _combine_call(p1, p2, x_flat, info_flat, gate_rows, g_final, y_sorted, tokens_per_batch, final_norm):
    t, d = x_flat.shape
    tc = GATHER_TILE
    per_b = tokens_per_batch // tc
    idx = pl.BlockSpec((tc,), lambda i: (i,), memory_space=pltpu.SMEM)
    return pl.pallas_call(
        functools.partial(_combine_kernel, final_norm=final_norm),
        grid=(t // tc,),
        in_specs=[idx, idx, pl.BlockSpec((tc, d), lambda i: (i, 0)), pl.BlockSpec((tc, LANES), lambda i: (i, 0)),
                  pl.BlockSpec((1, 1, d), lambda i: (i // per_b, 0, 0)), pl.BlockSpec((1, d), lambda i: (0, 0)),
                  pl.BlockSpec(memory_space=pl.ANY)],
        out_specs=pl.BlockSpec((tc, d), lambda i: (i, 0)),
        out_shape=jax.ShapeDtypeStruct((t, d), F32),
        scratch_shapes=[pltpu.VMEM((tc, d // 2), jnp.uint32), pltpu.VMEM((tc, d // 2), jnp.uint32),
                        pltpu.SemaphoreType.DMA(()), pltpu.SemaphoreType.DMA(())],
        name="moe_combine",
    )(p1, p2, x_flat, info_flat, gate_rows, g_final, y_sorted)


def _routing_tables(expert_ids, n_tokens):
    tm = EXPERT_ROW_TILE
    n_assign = n_tokens * TOP_K
    n_rows = n_assign + N_EXPERTS * tm
    e_flat = expert_ids.reshape(n_assign)
    onehot = (e_flat[:, None] == jnp.arange(N_EXPERTS, dtype=jnp.int32)[None, :]).astype(jnp.int32)
    rank = jnp.sum((jnp.cumsum(onehot, axis=0) - onehot) * onehot, axis=1)
    counts = jnp.sum(onehot, axis=0)
    padded = ((counts + tm - 1) // tm) * tm
    ends = jnp.cumsum(padded)
    starts = ends - padded
    pos = (starts[e_flat] + rank).astype(jnp.int32)
    row_token = jnp.zeros((n_rows,), jnp.int32).at[pos].set(jnp.arange(n_assign, dtype=jnp.int32) // TOP_K)
    n_valid = (ends[-1] // tm).astype(jnp.int32).reshape(1)
    tile_start = jnp.arange(n_rows // tm, dtype=jnp.int32) * tm
    tile_expert = jnp.minimum(jnp.sum((ends[None, :] <= tile_start[:, None]).astype(jnp.int32), axis=1), N_EXPERTS - 1)
    pos2 = pos.reshape(n_tokens, TOP_K)
    return row_token, tile_expert, n_valid, pos2[:, 0], pos2[:, 1]


def _prep_w_in(w_in):
    d = w_in.shape[0]
    cq, ckv, kr, qsb, ksb, vsb = jnp.split(
        w_in, (Q_LORA, Q_LORA + KV_LORA, Q_LORA + KV_LORA + MLA_ROPE,
               Q_LORA + KV_LORA + MLA_ROPE + SB_WIDTH, Q_LORA + KV_LORA + MLA_ROPE + 2 * SB_WIDTH), axis=1)
    z = lambda n: jnp.zeros((d, n), w_in.dtype)
    kr_blk = jnp.concatenate([z(MLA_NOPE), kr, z(LANES - MLA_QK_DIM)], axis=1)
    kr_swap = jnp.concatenate([z(MLA_NOPE), kr[:, HALF_ROPE:], kr[:, :HALF_ROPE], z(LANES - MLA_QK_DIM)], axis=1)
    return jnp.concatenate([cq, ckv, kr_blk, kr_swap, qsb * SB_HEAD_DIM ** -0.5, ksb, vsb], axis=1).astype(BF16)


def _prep_w_uq(w_uq):
    r = w_uq.shape[0]
    w = w_uq.reshape(r, MLA_HEADS, MLA_QK_DIM)
    nope, x1, x2 = w[..., :MLA_NOPE], w[..., MLA_NOPE:MLA_NOPE + HALF_ROPE], w[..., MLA_NOPE + HALF_ROPE:]
    pad = jnp.zeros((r, MLA_HEADS, LANES - MLA_QK_DIM), w.dtype)
    plain = jnp.concatenate([nope, x1, x2, pad], axis=-1).reshape(r, MLA_PAD_WIDTH)
    swapped = jnp.concatenate([jnp.zeros_like(nope), x2, x1, pad], axis=-1).reshape(r, MLA_PAD_WIDTH)
    return jnp.concatenate([plain, swapped], axis=1).astype(BF16)


def _prep_w_ukv(w_ukv):
    r = w_ukv.shape[0]
    w = w_ukv.reshape(r, MLA_HEADS, MLA_NOPE + MLA_V)
    k_nope, v = w[..., :MLA_NOPE], w[..., MLA_NOPE:]
    k_pad = jnp.concatenate([k_nope, jnp.zeros((r, MLA_HEADS, LANES - MLA_NOPE), w.dtype)], axis=-1)
    return jnp.concatenate([k_pad.reshape(r, MLA_PAD_WIDTH), v.reshape(r, MLA_WIDTH)], axis=1).astype(BF16)


def kernel(x, c, positions, w_ada, b_ada, g_mix, g_ffn, w_in, g_q_lat, w_uq, g_kv_lat, w_ukv, g_grp_mla, g_grp_sb, w_out, w_ff_gate, w_ff_up, w_ff_down, w_router, b_router, w_ex_gate, w_ex_up, w_ex_down, g_final):
    b, s, d = x.shape
    depth = w_ada.shape[0]
    n_tokens = b * s
    assert depth % 2 == 0, "the final norm is fused into the routed mixer of the last layer"

    c_pad = jnp.zeros((8, d), F32).at[:b].set(c)
    mod_all = _ada_call(c_pad, w_ada, b_ada)[:, :b].reshape(depth, b, N_MOD, d)

    lane = jnp.arange(LANES)
    freq = ROPE_BASE ** (-((lane - MLA_NOPE) % HALF_ROPE).astype(F32) / HALF_ROPE)
    pos_lanes = jnp.broadcast_to(positions.astype(F32)[..., None], (b, s, LANES))
    cos_t, sin_t = _rope_tables(pos_lanes, freq.reshape(1, LANES))

    row2 = lambda v: v.reshape(1, -1)
    for layer in range(depth):
        mod = mod_all[layer]
        qm, km, vtm, qs, ks, vts = _proj_call(
            x, mod, row2(g_mix[layer]), cos_t, sin_t, _prep_w_in(w_in[layer]), row2(g_q_lat[layer]),
            _prep_w_uq(w_uq[layer]), row2(g_kv_lat[layer]), _prep_w_ukv(w_ukv[layer]))
        o_mla = _mla_call(qm, km, vtm)
        o_sb = _sb_call(qs, ks, vts)
        x = _merge_call(x, o_mla, o_sb, mod, row2(g_grp_mla[layer]), row2(g_grp_sb[layer]), w_out[layer].astype(BF16))

        i = layer // 2
        if layer % 2 == 0:
            x = _ffn_call(x, mod, row2(g_ffn[layer]), w_ff_gate[i].astype(BF16), w_ff_up[i].astype(BF16),
                          w_ff_down[i].astype(BF16))
        else:
            wr = jnp.zeros((d, LANES), F32).at[:, :N_EXPERTS].set(w_router[i])
            wr_hi = wr.astype(BF16)
            wr_lo = (wr - wr_hi.astype(F32)).astype(BF16)
            br = jnp.zeros((1, LANES), F32).at[0, :N_EXPERTS].set(b_router[i])
            h, info = _router_call(x, mod, row2(g_ffn[layer]), wr_hi, wr_lo, br)
            info_flat = info.reshape(n_tokens, LANES)
            expert_ids = info_flat[:, :TOP_K].astype(jnp.int32)
            row_token, tile_expert, n_valid, p1, p2 = _routing_tables(expert_ids, n_tokens)
            xs = _gather_call(row_token, h.reshape(n_tokens, d // 2))
            ys = _expert_call(tile_expert, n_valid, xs, w_ex_gate, w_ex_up, w_ex_down, i)
            final = layer == depth - 1
            x = _combine_call(p1, p2, x.reshape(n_tokens, d), info_flat, mod[:, 5:6, :], row2(g_final), ys, s,
                              final_norm=final).reshape(b, s, d)
    return x
```
